```python
import math
import jax, jax.numpy as jnp
from jax import lax
import numpy as np

D_MODEL = 1024
BATCH = 8
SEQ = 2048
DEPTH = 1
DEC_BATCH = 128
DEC_SEQ = 1
PAST_LEN = 16384
PAGE_SIZE = 128

M_HEADS = 8
M_DK = 64
M_DV = 128
M_QK = M_HEADS * M_DK
M_V = M_HEADS * M_DV
M_CHUNK = 64
CONV_W = 4
M_INIT = -1e30
G_GROUPS = 8
G_DIM = 128
G_WIDTH = G_GROUPS * G_DIM
G_CHUNK = 128
N_EXPERTS = 32
TOP_K = 4
D_FF = 1024
SWIGLU_LIMIT = 7.0
SWIGLU_ALPHA = 1.702
RMS_EPS = 1e-6
D_IN = 2 * M_QK + M_V + 2 * M_HEADS + M_V + 2 * G_WIDTH + 2 * D_MODEL

kernel_name = 'hybrid_mlstm_gmlp_moe_step'


def rmsnorm(x, g):
    xf = x.astype(jnp.float32)
    y = xf * lax.rsqrt(jnp.mean(xf * xf, axis=-1, keepdims=True) + RMS_EPS)
    return (y * g.astype(jnp.float32)).astype(x.dtype)


def causal_conv(x, buf, w, b):
    T = x.shape[1]
    xp = jnp.concatenate([buf.astype(x.dtype), x], axis=1)
    y = b
    for j in range(CONV_W):
        y = y + w[j] * xp[:, j:j + T]
    return y, xp[:, T:]


def mlstm_chunk(q, k, v, ig, lf, C, n, m):
    L = q.shape[2]
    b = jnp.cumsum(lf, axis=-1)
    causal = jnp.tril(jnp.ones((L, L), dtype=bool))
    dmat = jnp.where(causal, b[..., :, None] - b[..., None, :] + ig[..., None, :], -jnp.inf)
    inter = b + m[..., None]
    m_t = jnp.maximum(inter, jnp.max(dmat, axis=-1))
    w_inter = jnp.exp(inter - m_t)
    s = jnp.einsum('bhtd,bhsd->bhts', q, k) * jnp.exp(dmat - m_t[..., None])
    num = w_inter[..., None] * jnp.einsum('bhtd,bhde->bhte', q, C) + jnp.einsum('bhts,bhse->bhte', s, v)
    den = w_inter * jnp.einsum('bhtd,bhd->bht', q, n) + jnp.sum(s, axis=-1)
    h = num / jnp.maximum(jnp.abs(den), jnp.exp(-m_t))[..., None]
    m_new = m_t[..., -1]
    decay = jnp.exp(b[..., -1] + m - m_new)
    w_s = jnp.exp(b[..., -1:] - b + ig - m_new[..., None])
    C_new = decay[..., None, None] * C + jnp.einsum('bhs,bhsd,bhse->bhde', w_s, k, v)
    n_new = decay[..., None] * n + jnp.einsum('bhs,bhsd->bhd', w_s, k)
    return h, C_new, n_new, m_new


def mlstm_seq(q, k, v, ig, lf, C, n, m, chunk):
    B, H, T, _ = q.shape
    nc = T // chunk
    def blocks(a):
        return jnp.moveaxis(a.reshape((B, H, nc, chunk) + a.shape[3:]), 2, 0)
    def step(carry, xs):
        Cc, nn_, mm = carry
        qc, kc, vc, ic, fc = xs
        h, Cc, nn_, mm = mlstm_chunk(qc, kc, vc, ic, fc, Cc, nn_, mm)
        return (Cc, nn_, mm), h
    (C, n, m), hs = lax.scan(step, (C, n, m), (blocks(q), blocks(k), blocks(v), blocks(ig), blocks(lf)))
    hs = jnp.moveaxis(hs, 0, 2).reshape(B, H, T, M_DV)
    return hs, C, n, m


def moe(h, w_router, b_router, w_gate, b_gate, w_up, b_up, w_down, b_down):
    B, T, D = h.shape
    hf = h.reshape(B * T, D)
    logits = (hf @ w_router + b_router).astype(jnp.float32)
    top_val, top_idx = lax.top_k(logits, TOP_K)
    wts = jax.nn.softmax(top_val, axis=-1)
    combine = jnp.sum(jax.nn.one_hot(top_idx, N_EXPERTS, dtype=jnp.float32) * wts[..., None], axis=1)
    out = jnp.zeros_like(hf)
    for e in range(N_EXPERTS):
        gate = jnp.minimum(hf @ w_gate[e] + b_gate[e], SWIGLU_LIMIT)
        up = jnp.clip(hf @ w_up[e] + b_up[e], -SWIGLU_LIMIT, SWIGLU_LIMIT)
        glu = gate * jax.nn.sigmoid(SWIGLU_ALPHA * gate)
        o = ((up + 1.0) * glu) @ w_down[e] + b_down[e]
        out = out + combine[:, e:e + 1].astype(hf.dtype) * o
    return out.reshape(B, T, D)


def decoder_layer(x, conv_buf, C, n, m, m_chunk, g_chunk,
                  g_mix, w_in, conv_w, conv_b, b_igate, b_fgate, g_mhnorm, w_a,
                  w_spatial, b_spatial, g_vnorm, w_b, w_out, g_ffn,
                  w_router, b_router, w_gate, b_gate, w_up, b_up, w_down, b_down):
    B, T, _ = x.shape
    f32 = jnp.float32
    hn = rmsnorm(x, g_mix)
    z = hn @ w_in
    o0 = 0
    def take(width):
        nonlocal o0
        part = z[..., o0:o0 + width]
        o0 += width
        return part
    qk_pre = take(2 * M_QK)
    v_pre = take(M_V)
    i_pre = take(M_HEADS)
    f_pre = take(M_HEADS)
    o_pre = take(M_V)
    u_pre = take(G_WIDTH)
    vg_pre = take(G_WIDTH)
    ga = take(D_MODEL)
    gb = take(D_MODEL)

    qk, conv_new = causal_conv(qk_pre, conv_buf, conv_w, conv_b)
    qk = jax.nn.silu(qk).astype(f32)
    q = qk[..., :M_QK].reshape(B, T, M_HEADS, M_DK).transpose(0, 2, 1, 3)
    k = qk[..., M_QK:].reshape(B, T, M_HEADS, M_DK).transpose(0, 2, 1, 3) * (M_DK ** -0.5)
    v = v_pre.astype(f32).reshape(B, T, M_HEADS, M_DV).transpose(0, 2, 1, 3)
    ig = (i_pre + b_igate).astype(f32).transpose(0, 2, 1)
    lf = jax.nn.log_sigmoid((f_pre + b_fgate).astype(f32)).transpose(0, 2, 1)
    hm, C_new, n_new, m_new = mlstm_seq(q, k, v, ig, lf, C.astype(f32), n.astype(f32), m.astype(f32), m_chunk)
    hm = hm.transpose(0, 2, 1, 3)
    hm = hm * lax.rsqrt(jnp.mean(hm * hm, axis=-1, keepdims=True) + RMS_EPS)
    hm = (hm.reshape(B, T, M_V) * g_mhnorm.astype(f32)).astype(x.dtype) * jax.nn.sigmoid(o_pre)
    y_a = hm @ w_a

    u = jax.nn.gelu(u_pre, approximate=False)
    vg = rmsnorm(jax.nn.gelu(vg_pre, approximate=False), g_vnorm).reshape(B, T, G_GROUPS, G_DIM)
    L = g_chunk
    vc = vg.reshape(B, T // L, L, G_GROUPS, G_DIM)
    ws = w_spatial[:, :L, :L] * jnp.tril(jnp.ones((L, L), dtype=w_spatial.dtype))
    s = jnp.einsum('gts,bcsgd->bctgd', ws, vc) + b_spatial[:, :L].T[None, None, :, :, None]
    y_b = (u * s.reshape(B, T, G_WIDTH)) @ w_b

    x = x + (jax.nn.sigmoid(ga) * y_a + jax.nn.sigmoid(gb) * y_b) @ w_out
    x = x + moe(rmsnorm(x, g_ffn), w_router, b_router, w_gate, b_gate, w_up, b_up, w_down, b_down)
    return x, conv_new, C_new, n_new, m_new, vg


def setup_inputs(seed: int = 0) -> dict:
    key = jax.random.key(seed)
    ks = jax.random.split(key, 32)
    nrm = jax.random.normal
    f32 = jnp.float32
    d = {}
    d['x_prompt'] = nrm(ks[0], (BATCH, SEQ, D_MODEL), f32)
    d['x_sample'] = nrm(ks[1], (DEC_BATCH, DEC_SEQ, D_MODEL), f32)
    d['state_conv'] = nrm(ks[2], (DEPTH, DEC_BATCH, CONV_W - 1, 2 * M_QK), f32)
    d['state_C'] = 0.1 * nrm(ks[3], (DEPTH, DEC_BATCH, M_HEADS, M_DK, M_DV), f32)
    d['state_n'] = 0.5 * jnp.abs(nrm(ks[4], (DEPTH, DEC_BATCH, M_HEADS, M_DK), f32))
    d['state_m'] = nrm(ks[5], (DEPTH, DEC_BATCH, M_HEADS), f32)
    d['g_mix'] = 1.0 + 0.01 * nrm(ks[6], (DEPTH, D_MODEL), f32)
    d['w_in'] = nrm(ks[7], (DEPTH, D_MODEL, D_IN), f32) * D_MODEL ** -0.5
    d['conv_w'] = nrm(ks[8], (DEPTH, CONV_W, 2 * M_QK), f32) * CONV_W ** -0.5
    d['conv_b'] = 0.01 * nrm(ks[9], (DEPTH, 2 * M_QK), f32)
    d['b_igate'] = -1.0 + 0.1 * nrm(ks[10], (DEPTH, M_HEADS), f32)
    d['b_fgate'] = 3.0 + 0.5 * nrm(ks[11], (DEPTH, M_HEADS), f32)
    d['g_mhnorm'] = 1.0 + 0.01 * nrm(ks[12], (DEPTH, M_V), f32)
    d['w_a'] = nrm(ks[13], (DEPTH, M_V, D_MODEL), f32) * M_V ** -0.5
    d['w_spatial'] = nrm(ks[14], (DEPTH, G_GROUPS, G_CHUNK, G_CHUNK), f32) * G_CHUNK ** -0.5
    d['b_spatial'] = 1.0 + 0.1 * nrm(ks[15], (DEPTH, G_GROUPS, G_CHUNK), f32)
    d['g_vnorm'] = 1.0 + 0.01 * nrm(ks[16], (DEPTH, G_WIDTH), f32)
    d['w_b'] = nrm(ks[17], (DEPTH, G_WIDTH, D_MODEL), f32) * G_WIDTH ** -0.5
    d['w_out'] = nrm(ks[18], (DEPTH, D_MODEL, D_MODEL), f32) * D_MODEL ** -0.5
    d['g_ffn'] = 1.0 + 0.01 * nrm(ks[19], (DEPTH, D_MODEL), f32)
    d['w_router'] = nrm(ks[20], (DEPTH, D_MODEL, N_EXPERTS), f32) * D_MODEL ** -0.5
    d['b_router'] = 0.01 * nrm(ks[21], (DEPTH, N_EXPERTS), f32)
    d['w_gate'] = nrm(ks[22], (DEPTH, N_EXPERTS, D_MODEL, D_FF), f32) * D_MODEL ** -0.5
    d['b_gate'] = 0.01 * nrm(ks[23], (DEPTH, N_EXPERTS, D_FF), f32)
    d['w_up'] = nrm(ks[24], (DEPTH, N_EXPERTS, D_MODEL, D_FF), f32) * D_MODEL ** -0.5
    d['b_up'] = 0.01 * nrm(ks[25], (DEPTH, N_EXPERTS, D_FF), f32)
    d['w_down'] = nrm(ks[26], (DEPTH, N_EXPERTS, D_FF, D_MODEL), f32) * D_FF ** -0.5
    d['b_down'] = 0.01 * nrm(ks[27], (DEPTH, N_EXPERTS, D_MODEL), f32)
    d['g_final'] = 1.0 + 0.01 * nrm(ks[28], (D_MODEL,), f32)
    return d


def reference(x_prompt, x_sample, state_conv, state_C, state_n, state_m,
              g_mix, w_in, conv_w, conv_b, b_igate, b_fgate, g_mhnorm, w_a,
              w_spatial, b_spatial, g_vnorm, w_b, w_out, g_ffn,
              w_router, b_router, w_gate, b_gate, w_up, b_up, w_down, b_down, g_final):
    f32 = jnp.float32
    xp, xs = x_prompt, x_sample
    p_conv, p_C, p_n, p_m = [], [], [], []
    s_conv, s_C, s_n, s_m, s_v = [], [], [], [], []
    for l in range(DEPTH):
        w = (g_mix[l], w_in[l], conv_w[l], conv_b[l], b_igate[l], b_fgate[l], g_mhnorm[l], w_a[l],
             w_spatial[l], b_spatial[l], g_vnorm[l], w_b[l], w_out[l], g_ffn[l],
             w_router[l], b_router[l], w_gate[l], b_gate[l], w_up[l], b_up[l], w_down[l], b_down[l])
        conv0 = jnp.zeros((BATCH, CONV_W - 1, 2 * M_QK), xp.dtype)
        C0 = jnp.zeros((BATCH, M_HEADS, M_DK, M_DV), f32)
        n0 = jnp.zeros((BATCH, M_HEADS, M_DK), f32)
        m0 = jnp.full((BATCH, M_HEADS), M_INIT, f32)
        xp, pc, pC, pn, pm, _ = decoder_layer(xp, conv0, C0, n0, m0, min(M_CHUNK, SEQ), G_CHUNK, *w)
        p_conv.append(pc); p_C.append(pC); p_n.append(pn); p_m.append(pm)
        xs, sc, sC, sn, sm, sv = decoder_layer(xs, state_conv[l], state_C[l], state_n[l], state_m[l],
                                               DEC_SEQ, DEC_SEQ, *w)
        s_conv.append(sc); s_C.append(sC); s_n.append(sn); s_m.append(sm); s_v.append(sv)
    y_prompt = rmsnorm(xp, g_final)
    y_sample = rmsnorm(xs, g_final)
    return (y_prompt, y_sample,
            jnp.stack(p_conv), jnp.stack(p_C), jnp.stack(p_n), jnp.stack(p_m),
            jnp.stack(s_conv), jnp.stack(s_C), jnp.stack(s_n), jnp.stack(s_m), jnp.stack(s_v))
```

```python
import functools

import jax
import jax.numpy as jnp
from jax import lax
from jax.experimental import pallas as pl
from jax.experimental.pallas import tpu as pltpu

F32 = jnp.float32
BF16 = jnp.bfloat16
I32 = jnp.int32
HIGHEST = lax.Precision.HIGHEST

N_HEADS = 8
D_K = 64
D_V = 128
D_QK = N_HEADS * D_K
D_VAL = N_HEADS * D_V
CONV_TAPS = 4
GATE_COLS = 2 * N_HEADS
N_GROUPS = 8
GROUP_DIM = 128
SPATIAL_CHUNK = 128
N_EXPERTS = 32
TOP_K = 4
SWIGLU_LIMIT = 7.0
SWIGLU_ALPHA = 1.702
RMS_EPS = 1e-6
EMPTY_MAX = -1e30
LANES = 128
SUBLANES = 8
D_MODEL = 1024
ROW_TILES = D_MODEL // LANES

PROMPT_BLOCK = 256
SAMPLE_STATE_BLOCK = 8
TOKEN_BLOCK = PROMPT_BLOCK
EXPERT_TILE = 512
VMEM_LIMIT = 56 * 1024 * 1024


def _dot(a, b, precision=None):
    return jnp.dot(a, b, preferred_element_type=F32, precision=precision)


def _dot_nt(a, b, precision=None):
    return lax.dot_general(a, b, (((1,), (1,)), ((), ())), preferred_element_type=F32, precision=precision)


def _dot_tn(a, b, precision=None):
    return lax.dot_general(a, b, (((0,), (0,)), ((), ())), preferred_element_type=F32, precision=precision)


def _rms(x, g):
    return x * lax.rsqrt(jnp.mean(x * x, axis=-1, keepdims=True) + RMS_EPS) * g


def _sigmoid(x):
    return jax.nn.sigmoid(x)


def _gelu(x):
    return 0.5 * x * (1.0 + lax.erf(x * (2.0 ** -0.5)))


def _log_sigmoid(x):
    return jax.nn.log_sigmoid(x)


def _store_rows(ref, val):
    rows = val.shape[0]
    for jt in range(ROW_TILES):
        ref[pl.ds(jt, rows, stride=ROW_TILES), :] = val[:, jt * LANES:(jt + 1) * LANES]


def _load_rows(ref, rows):
    return jnp.concatenate([ref[pl.ds(jt, rows, stride=ROW_TILES), :] for jt in range(ROW_TILES)], axis=1)


def _row_tile(ref, r):
    return ref.at[pl.ds(pl.multiple_of(r * ROW_TILES, ROW_TILES), ROW_TILES)]


def _in_projection(hn, w_main_ref, piece):
    return _dot(hn, w_main_ref[:, piece * 1024:(piece + 1) * 1024])


def _post_mixer(x, hm, z_o, z_u, z_vg, z_ga, z_gb, spatial_fn,
                g_mh_ref, w_a_ref, g_vn_ref, w_b_ref, w_out_ref, g_ffn_ref):
    parts = []
    for h in range(N_HEADS):
        hh = hm[:, h * D_V:(h + 1) * D_V]
        parts.append(hh * lax.rsqrt(jnp.mean(hh * hh, axis=-1, keepdims=True) + RMS_EPS))
    hmn = jnp.concatenate(parts, axis=1) * g_mh_ref[...]
    y_a = _dot((hmn * _sigmoid(z_o)).astype(BF16), w_a_ref[...])
    u = _gelu(z_u)
    vg = _rms(_gelu(z_vg), g_vn_ref[...])
    s = spatial_fn(vg)
    y_b = _dot((u * s).astype(BF16), w_b_ref[...])
    merged = _sigmoid(z_ga) * y_a + _sigmoid(z_gb) * y_b
    x1 = x + _dot(merged.astype(BF16), w_out_ref[...])
    h2 = _rms(x1, g_ffn_ref[...])
    return x1, h2, vg


def _prompt_mixer_kernel(x_ref, x1s_ref, h2s_ref, *refs, blocks_per_seq, n_prompt_blocks):
    g = pl.program_id(0)
    x1_ref, h2_ref = refs[18], refs[19]

    @pl.when(g < n_prompt_blocks)
    def _():
        _prompt_block(g % blocks_per_seq, x_ref, *refs)

    @pl.when(g == n_prompt_blocks)
    def _():
        ns = x1s_ref.shape[0]
        x1_ref[0:ns, :] = x1s_ref[...]
        x1_ref[ns:, :] = jnp.zeros((x1_ref.shape[0] - ns, x1_ref.shape[1]), F32)
        h2_ref[0:ns * ROW_TILES, :] = h2s_ref[...]
        h2_ref[ns * ROW_TILES:, :] = jnp.zeros((h2_ref.shape[0] - ns * ROW_TILES, LANES), F32)


def _prompt_block(j, x_ref, g_mix_ref, w_main_ref, w_ig_ref, w_fg_ref, w_ift_ref, b_ig_ref, b_fg_ref,
                  gate_b_col_ref, conv_w_ref, conv_b_ref, g_mh_ref, w_a_ref, ws_ref, bsp_ref, g_vn_ref,
                  w_b_ref, w_out_ref, g_ffn_ref,
                  x1_ref, h2_ref, pconv_ref, pc_ref, pn_ref, pm_ref,
                  xp_ref):
    tb = x_ref.shape[1]

    @pl.when(j == 0)
    def _():
        pc_ref[...] = jnp.zeros_like(pc_ref)
        pn_ref[...] = jnp.zeros_like(pn_ref)
        pm_ref[...] = jnp.full(pm_ref.shape, EMPTY_MAX, F32)
        xp_ref[0:SUBLANES, :] = jnp.zeros((SUBLANES, xp_ref.shape[1]), F32)

    x = x_ref[0]
    hn = _rms(x, g_mix_ref[...]).astype(BF16)

    z_qk = _in_projection(hn, w_main_ref, 0)
    xp_ref[SUBLANES:SUBLANES + tb, :] = z_qk
    y = conv_b_ref[...] + conv_w_ref[CONV_TAPS - 1:CONV_TAPS, :] * z_qk
    for tap in range(CONV_TAPS - 1):
        back = CONV_TAPS - 1 - tap
        y = y + conv_w_ref[tap:tap + 1, :] * xp_ref[SUBLANES - back:SUBLANES - back + tb, :]
    tail = z_qk[tb - (CONV_TAPS - 1):tb, :]
    xp_ref[SUBLANES - (CONV_TAPS - 1):SUBLANES, :] = tail
    pconv_ref[0] = tail
    qk = y * _sigmoid(y)
    q = qk[:, :D_QK]
    k = qk[:, D_QK:] * (D_K ** -0.5)
    qb = q.astype(BF16)
    v = _in_projection(hn, w_main_ref, 1)
    vb = v.astype(BF16)

    ig_c = _dot(hn, w_ig_ref[...]) + b_ig_ref[...]
    lf_c = _log_sigmoid(_dot(hn, w_fg_ref[...]) + b_fg_ref[...])
    gr = _dot_nt(w_ift_ref[...], hn) + gate_b_col_ref[:, 0:1]
    ig_r = gr[0:N_HEADS, :]
    lf_r = _log_sigmoid(gr[N_HEADS:GATE_COLS, :])
    row_i = lax.broadcasted_iota(I32, (tb, tb), 0)
    col_i = lax.broadcasted_iota(I32, (tb, tb), 1)
    causal = col_i <= row_i
    tri = causal.astype(F32)
    b_c = _dot(tri, lf_c, HIGHEST)
    b_r = _dot_nt(lf_r, tri, HIGHEST)

    hm_parts = []
    for h in range(N_HEADS):
        bt = b_c[:, h:h + 1]
        ig_col = ig_c[:, h:h + 1]
        bs = b_r[h:h + 1, :]
        ig_row = ig_r[h:h + 1, :]
        m_prev = pm_ref[0, h:h + 1, 0:1]
        c_prev = pc_ref[0, h]
        n_prev = pn_ref[0, h:h + 1, :]
        q_h = q[:, h * D_K:(h + 1) * D_K]
        qb_h = qb[:, h * D_K:(h + 1) * D_K]
        k_h = k[:, h * D_K:(h + 1) * D_K]
        vb_h = vb[:, h * D_V:(h + 1) * D_V]

        dmat = jnp.where(causal, bt - bs + ig_row, -jnp.inf)
        inter = bt + m_prev
        m_t = jnp.maximum(inter, jnp.max(dmat, axis=-1, keepdims=True))
        w_inter = jnp.exp(inter - m_t)
        s = _dot_nt(qb_h, k_h.astype(BF16)) * jnp.exp(dmat - m_t)
        num = w_inter * _dot(qb_h, c_prev.astype(BF16)) + _dot(s.astype(BF16), vb_h)
        den = w_inter * jnp.sum(q_h * n_prev, axis=-1, keepdims=True) + jnp.sum(s, axis=-1, keepdims=True)
        hm_parts.append(num / jnp.maximum(jnp.abs(den), jnp.exp(-m_t)))

        m_new = m_t[tb - 1:tb, :]
        b_last = bt[tb - 1:tb, :]
        decay = jnp.exp(b_last + m_prev - m_new)
        w_s = jnp.exp(b_last - bt + ig_col - m_new)
        kw = k_h * w_s
        pc_ref[0, h] = decay * c_prev + _dot_tn(kw.astype(BF16), vb_h)
        pn_ref[0, h:h + 1, :] = decay * n_prev + jnp.sum(kw, axis=0, keepdims=True)
        pm_ref[0, h:h + 1, :] = jnp.broadcast_to(m_new, (1, pm_ref.shape[2]))
    hm = jnp.concatenate(hm_parts, axis=1)

    def spatial(vg):
        vgb = vg.astype(BF16)
        rows = []
        for c in range(tb // SPATIAL_CHUNK):
            blocks = []
            for g in range(N_GROUPS):
                blk = vgb[c * SPATIAL_CHUNK:(c + 1) * SPATIAL_CHUNK, g * GROUP_DIM:(g + 1) * GROUP_DIM]
                blocks.append(_dot(ws_ref[g], blk))
            rows.append(jnp.concatenate(blocks, axis=1) + bsp_ref[...])
        return jnp.concatenate(rows, axis=0)

    z_o = _in_projection(hn, w_main_ref, 2)
    z_u = _in_projection(hn, w_main_ref, 3)
    z_vg = _in_projection(hn, w_main_ref, 4)
    z_ga = _in_projection(hn, w_main_ref, 5)
    z_gb = _in_projection(hn, w_main_ref, 6)
    x1, h2, _ = _post_mixer(x, hm, z_o, z_u, z_vg, z_ga, z_gb, spatial,
                            g_mh_ref, w_a_ref, g_vn_ref, w_b_ref, w_out_ref, g_ffn_ref)
    x1_ref[...] = x1
    _store_rows(h2_ref, h2)


def _whole_vmem():
    return pl.BlockSpec(memory_space=pltpu.VMEM)


def _prompt_mixer(x_prompt, x1_sample, h2_sample, wts):
    batch, seq, d = x_prompt.shape
    tb = PROMPT_BLOCK
    nt = seq // tb
    n_blocks = batch * nt
    n_total = (n_blocks + 1) * tb
    out_shapes = (
        jax.ShapeDtypeStruct((n_total, d), F32),
        jax.ShapeDtypeStruct((n_total * ROW_TILES, LANES), F32),
        jax.ShapeDtypeStruct((batch, CONV_TAPS - 1, 2 * D_QK), F32),
        jax.ShapeDtypeStruct((batch, N_HEADS, D_K, D_V), F32),
        jax.ShapeDtypeStruct((batch, N_HEADS, D_K), F32),
        jax.ShapeDtypeStruct((batch, N_HEADS, LANES), F32),
    )
    seq_of = lambda g: jnp.minimum(g, n_blocks - 1) // nt
    row_map = lambda g: (g, 0)
    out_specs = (
        pl.BlockSpec((tb, d), row_map),
        pl.BlockSpec((tb * ROW_TILES, LANES), row_map),
        pl.BlockSpec((1, CONV_TAPS - 1, 2 * D_QK), lambda g: (seq_of(g), 0, 0)),
        pl.BlockSpec((1, N_HEADS, D_K, D_V), lambda g: (seq_of(g), 0, 0, 0)),
        pl.BlockSpec((1, N_HEADS, D_K), lambda g: (seq_of(g), 0, 0)),
        pl.BlockSpec((1, N_HEADS, LANES), lambda g: (seq_of(g), 0, 0)),
    )
    in_specs = ([pl.BlockSpec((1, tb, d), lambda g: (seq_of(g), jnp.minimum(g, n_blocks - 1) % nt, 0))]
                + [_whole_vmem()] * 20)
    return pl.pallas_call(
        functools.partial(_prompt_mixer_kernel, blocks_per_seq=nt, n_prompt_blocks=n_blocks),
        grid=(n_blocks + 1,),
        in_specs=in_specs,
        out_specs=out_specs,
        out_shape=out_shapes,
        scratch_shapes=[pltpu.VMEM((SUBLANES + tb, 2 * D_QK), F32)],
        compiler_params=pltpu.CompilerParams(dimension_semantics=("arbitrary",), vmem_limit_bytes=VMEM_LIMIT),
        name="prompt_mixer",
    )(x_prompt, x1_sample, h2_sample,
      wts["g_mix"], wts["w_main"], wts["w_ig"], wts["w_fg"], wts["w_ift"], wts["b_ig"], wts["b_fg"],
      wts["gate_b_col"], wts["conv_w"], wts["conv_b"], wts["g_mh"], wts["w_a"], wts["ws_masked"], wts["bsp_full"],
      wts["g_vn"], wts["w_b"], wts["w_out"], wts["g_ffn"])


def _sample_pre_kernel(x_ref, c0_ref, c1_ref, c2_ref, n_ref, m_ref,
                       g_mix_ref, w_main_ref, w_ig_ref, w_fg_ref, b_ig_ref, b_fg_ref, conv_w_ref, conv_b_ref,
                       seg_ref, segt_ref,
                       qw_ref, kw_ref, v_ref, scal_ref, nnew_ref, zqk_ref, rest_ref):
    x = x_ref[...]
    hn = _rms(x, g_mix_ref[...]).astype(BF16)
    z_qk = _in_projection(hn, w_main_ref, 0)
    zqk_ref[...] = z_qk
    y = (conv_b_ref[...] + conv_w_ref[0:1, :] * c0_ref[...] + conv_w_ref[1:2, :] * c1_ref[...]
         + conv_w_ref[2:3, :] * c2_ref[...] + conv_w_ref[3:4, :] * z_qk)
    qk = y * _sigmoid(y)
    q = qk[:, :D_QK]
    k = qk[:, D_QK:] * (D_K ** -0.5)
    v_ref[...] = _in_projection(hn, w_main_ref, 1)
    for p in range(5):
        rest_ref[:, p * 1024:(p + 1) * 1024] = _in_projection(hn, w_main_ref, 2 + p)

    ig = _dot(hn, w_ig_ref[...]) + b_ig_ref[...]
    lf = _log_sigmoid(_dot(hn, w_fg_ref[...]) + b_fg_ref[...])
    m_prev = m_ref[...]
    m_new = jnp.maximum(lf + m_prev, ig)
    decay = jnp.exp(lf + m_prev - m_new)
    w_s = jnp.exp(ig - m_new)
    n_prev = n_ref[...]
    seg = seg_ref[...]
    segt = segt_ref[...]
    qk_dot = _dot(q * k, seg, HIGHEST)
    qn_dot = _dot(q * n_prev, seg, HIGHEST)
    s = qk_dot * w_s
    den = decay * qn_dot + s
    denom = jnp.maximum(jnp.abs(den), jnp.exp(-m_new))
    decay_x = _dot(decay, segt, HIGHEST)
    ws_x = _dot(w_s, segt, HIGHEST)
    qw_ref[...] = q * decay_x
    kw = k * ws_x
    kw_ref[...] = kw
    nnew_ref[...] = decay_x * n_prev + kw
    scal_ref[0] = decay
    scal_ref[1] = s
    scal_ref[2] = denom
    scal_ref[3] = m_new


def _sample_state_kernel(c_ref, qw_ref, kw_ref, v_ref, dec_ref, s_ref, den_ref, cnew_ref, h_ref):
    c = c_ref[...]
    v = v_ref[...]
    cnew_ref[...] = dec_ref[...] * c + kw_ref[...] * v
    num = jnp.sum(qw_ref[...] * c, axis=2, keepdims=True) + s_ref[...] * v
    h_ref[...] = num / den_ref[...]


def _sample_post_kernel(x_ref, hm_ref, rest_ref,
                        wsp_ref, bsp_ref, g_mh_ref, w_a_ref, g_vn_ref, w_b_ref, w_out_ref, g_ffn_ref,
                        x1_ref, h2_ref, vg_ref):
    z =[rest_ref[:, p * 1024:(p + 1) * 1024] for p in range(5)]
    spatial = lambda vg: vg * wsp_ref[...] + bsp_ref[...]
    x1, h2, vg = _post_mixer(x_ref[...], hm_ref[...], z[0], z[1], z[2], z[3], z[4], spatial,
                             g_mh_ref, w_a_ref, g_vn_ref, w_b_ref, w_out_ref, g_ffn_ref)
    x1_ref[...] = x1
    _store_rows(h2_ref, h2)
    vg_ref[...] = vg


def _sample_mixer(x_sample, state_conv, state_c, state_n, state_m, wts):
    ns, _, d = x_sample.shape
    xs = x_sample.reshape(ns, d)
    c_taps = [state_conv[:, t, :] for t in range(CONV_TAPS - 1)]
    n_rows = state_n.reshape(ns, D_QK)
    pre_out = (
        jax.ShapeDtypeStruct((ns, D_QK), F32),
        jax.ShapeDtypeStruct((ns, D_QK), F32),
        jax.ShapeDtypeStruct((ns, D_VAL), F32),
        jax.ShapeDtypeStruct((4, ns, LANES), F32),
        jax.ShapeDtypeStruct((ns, D_QK), F32),
        jax.ShapeDtypeStruct((ns, 2 * D_QK), F32),
        jax.ShapeDtypeStruct((ns, 5 * 1024), F32),
    )
    qw, kw, v, scal, n_new, z_qk, rest = pl.pallas_call(
        _sample_pre_kernel,
        out_shape=pre_out,
        in_specs=[_whole_vmem()] * 16,
        out_specs=tuple(_whole_vmem() for _ in pre_out),
        compiler_params=pltpu.CompilerParams(vmem_limit_bytes=VMEM_LIMIT),
        name="sample_pre",
    )(xs, c_taps[0], c_taps[1], c_taps[2], n_rows, jnp.pad(state_m, ((0, 0), (0, LANES - N_HEADS))),
      wts["g_mix"], wts["w_main"], wts["w_ig"], wts["w_fg"], wts["b_ig"], wts["b_fg"], wts["conv_w"],
      wts["conv_b"], wts["seg"], wts["segt"])
    scal = scal[:, :, :N_HEADS]

    bb = SAMPLE_STATE_BLOCK
    col4 = lambda a: a.reshape(ns, N_HEADS, D_K, 1)
    sc4 = lambda a: a.reshape(ns, N_HEADS, 1, 1)
    blk = lambda *tail: pl.BlockSpec((bb,) + tail, lambda i: (i, 0, 0, 0))
    c_new, hm = pl.pallas_call(
        _sample_state_kernel,
        grid=(ns // bb,),
        in_specs=[blk(N_HEADS, D_K, D_V), blk(N_HEADS, D_K, 1), blk(N_HEADS, D_K, 1), blk(N_HEADS, 1, D_V),
                  blk(N_HEADS, 1, 1), blk(N_HEADS, 1, 1), blk(N_HEADS, 1, 1)],
        out_specs=(blk(N_HEADS, D_K, D_V), blk(N_HEADS, 1, D_V)),
        out_shape=(jax.ShapeDtypeStruct((ns, N_HEADS, D_K, D_V), F32),
                   jax.ShapeDtypeStruct((ns, N_HEADS, 1, D_V), F32)),
        compiler_params=pltpu.CompilerParams(dimension_semantics=("arbitrary",), vmem_limit_bytes=VMEM_LIMIT),
        name="sample_state",
    )(state_c, col4(qw), col4(kw), v.reshape(ns, N_HEADS, 1, D_V), sc4(scal[0]), sc4(scal[1]), sc4(scal[2]))

    post_out = (jax.ShapeDtypeStruct((ns, d), F32), jax.ShapeDtypeStruct((ns * ROW_TILES, LANES), F32),
                jax.ShapeDtypeStruct((ns, d), F32))
    x1_s, h2_s, vg = pl.pallas_call(
        _sample_post_kernel,
        out_shape=post_out,
        in_specs=[_whole_vmem()] * 11,
        out_specs=tuple(_whole_vmem() for _ in post_out),
        compiler_params=pltpu.CompilerParams(vmem_limit_bytes=VMEM_LIMIT),
        name="sample_post",
    )(xs, hm.reshape(ns, D_VAL), rest, wts["wsp_row"], wts["bsp_row"], wts["g_mh"], wts["w_a"], wts["g_vn"],
      wts["w_b"], wts["w_out"], wts["g_ffn"])
    s_conv = jnp.concatenate([state_conv[:, 1:, :], z_qk[:, None, :]], axis=1)
    return x1_s, h2_s, s_conv, c_new, n_new.reshape(ns, N_HEADS, D_K), scal[3], vg


def _router_kernel(h2_ref, w_rt_ref, b_r_ref, pos_ref, wt_ref, cnt_ref, run_ref, off_ref):
    phase = pl.program_id(0)
    i = pl.program_id(1)
    tb = h2_ref.shape[0] // ROW_TILES

    @pl.when(jnp.logical_and(phase == 0, i == 0))
    def _():
        cnt_ref[...] = jnp.zeros_like(cnt_ref)

    logits = _dot_nt(w_rt_ref[...], _load_rows(h2_ref, tb), HIGHEST) + b_r_ref[:, 0:1]
    e_iota = lax.broadcasted_iota(I32, logits.shape, 0)
    sels, vals = [], []
    work = logits
    for _ in range(TOP_K):
        mx = jnp.max(work, axis=0, keepdims=True)
        idx = jnp.min(jnp.where(work == mx, e_iota, N_EXPERTS), axis=0, keepdims=True)
        sel = e_iota == idx
        work = jnp.where(sel, -jnp.inf, work)
        sels.append(sel.astype(F32))
        vals.append(mx)
    sel_all = sels[0] + sels[1] + sels[2] + sels[3]
    block_count = jnp.sum(sel_all, axis=1, keepdims=True)

    @pl.when(phase == 0)
    def _():
        cnt_ref[...] += jnp.broadcast_to(block_count, cnt_ref.shape)

    @pl.when(jnp.logical_and(phase == 1, i == 0))
    def _():
        cnt = cnt_ref[...]
        padded = jnp.ceil(cnt * (1.0 / EXPERT_TILE)) * EXPERT_TILE
        r = lax.broadcasted_iota(I32, (N_EXPERTS, N_EXPERTS), 0)
        c = lax.broadcasted_iota(I32, (N_EXPERTS, N_EXPERTS), 1)
        off_ref[...] = _dot((c < r).astype(F32), padded, HIGHEST)
        run_ref[...] = jnp.zeros_like(run_ref)

    @pl.when(phase == 1)
    def _():
        r = lax.broadcasted_iota(I32, (tb, tb), 0)
        c = lax.broadcasted_iota(I32, (tb, tb), 1)
        before = _dot(sel_all.astype(BF16), (r < c).astype(BF16))
        row = before + run_ref[:, 0:1] + off_ref[:, 0:1]
        es = [jnp.exp(vv - vals[0]) for vv in vals]
        tot = es[0] + es[1] + es[2] + es[3]
        for kk in range(TOP_K):
            pos_ref[kk:kk + 1, :] = jnp.sum(sels[kk] * row, axis=0, keepdims=True).astype(I32)
            wt_ref[kk:kk + 1, :] = es[kk] / tot
        run_ref[...] += jnp.broadcast_to(block_count, run_ref.shape)


def _router(h2_all, wts):
    n = h2_all.shape[0] // ROW_TILES
    tb = TOKEN_BLOCK
    nb = n // tb
    return pl.pallas_call(
        _router_kernel,
        grid=(2, nb),
        in_specs=[pl.BlockSpec((tb * ROW_TILES, LANES), lambda p, i: (i, 0)), _whole_vmem(), _whole_vmem()],
        out_specs=(pl.BlockSpec((TOP_K, tb), lambda p, i: (0, i * p)),
                   pl.BlockSpec((TOP_K, tb), lambda p, i: (0, i * p)),
                   pl.BlockSpec((N_EXPERTS, LANES), lambda p, i: (0, 0))),
        out_shape=(jax.ShapeDtypeStruct((TOP_K, n), I32), jax.ShapeDtypeStruct((TOP_K, n), F32),
                   jax.ShapeDtypeStruct((N_EXPERTS, LANES), F32)),
        scratch_shapes=[pltpu.VMEM((N_EXPERTS, LANES), F32), pltpu.VMEM((N_EXPERTS, LANES), F32)],
        compiler_params=pltpu.CompilerParams(
            dimension_semantics=("arbitrary", "arbitrary"), vmem_limit_bytes=VMEM_LIMIT),
        name="router",
    )(h2_all, wts["w_rt"], wts["b_r_col"])


def _dispatch_kernel(clear_ref, pos_ref, h2_ref, xs_ref, zero_ref, sem, zsem):
    i = pl.program_id(0)
    tb = pos_ref.shape[1]
    tile_rows = EXPERT_TILE * ROW_TILES
    n_tiles = xs_ref.shape[0] // tile_rows

    @pl.when(i == 0)
    def _():
        zero_ref[...] = jnp.zeros_like(zero_ref)

        def tile_copy(t):
            start_row = pl.multiple_of(t * tile_rows, tile_rows)
            return pltpu.make_async_copy(zero_ref, xs_ref.at[pl.ds(start_row, tile_rows)], zsem)

        def start(t, carry):
            @pl.when(clear_ref[t] > 0)
            def _():
                tile_copy(t).start()
            return carry

        def wait(t, carry):
            @pl.when(clear_ref[t] > 0)
            def _():
                tile_copy(t).wait()
            return carry

        lax.fori_loop(0, n_tiles, start, 0)
        lax.fori_loop(0, n_tiles, wait, 0)

    def row_copy(t, kk):
        return pltpu.make_async_copy(_row_tile(h2_ref, i * tb + t), _row_tile(xs_ref, pos_ref[kk, t]), sem)

    def start_rows(t, carry):
        for kk in range(TOP_K):
            row_copy(t, kk).start()
        return carry

    def wait_rows(t, carry):
        for kk in range(TOP_K):
            row_copy(t, kk).wait()
        return carry

    lax.fori_loop(0, tb, start_rows, 0)
    lax.fori_loop(0, tb, wait_rows, 0)


def _dispatch(h2_all, pos, tail_tile, n_rows):
    n = h2_all.shape[0] // ROW_TILES
    tb = TOKEN_BLOCK
    return pl.pallas_call(
        _dispatch_kernel,
        grid_spec=pltpu.PrefetchScalarGridSpec(
            num_scalar_prefetch=1,
            grid=(n // tb,),
            in_specs=[pl.BlockSpec((TOP_K, tb), lambda i, tail: (0, i), memory_space=pltpu.SMEM),
                      pl.BlockSpec(memory_space=pl.ANY)],
            out_specs=pl.BlockSpec(memory_space=pl.ANY),
            scratch_shapes=[pltpu.VMEM((EXPERT_TILE * ROW_TILES, LANES), F32), pltpu.SemaphoreType.DMA,
                            pltpu.SemaphoreType.DMA],
        ),
        out_shape=jax.ShapeDtypeStruct((n_rows * ROW_TILES, LANES), F32),
        compiler_params=pltpu.CompilerParams(dimension_semantics=("arbitrary",), vmem_limit_bytes=VMEM_LIMIT),
        name="dispatch",
    )(tail_tile, pos, h2_all)


def _expert_kernel(tile_e_ref, nvalid_ref, xs_ref, wg_ref, bg_ref, wu_ref, bu_ref, wd_ref, bd_ref, ys_ref,
                   wg_b, wu_b, wd_b):
    i = pl.program_id(0)
    prev = tile_e_ref[jnp.maximum(i - 1, 0)]
    fresh = jnp.logical_or(i == 0, tile_e_ref[i] != prev)

    @pl.when(jnp.logical_and(i < nvalid_ref[0], fresh))
    def _():
        wg_b[...] = wg_ref[0].astype(BF16)
        wu_b[...] = wu_ref[0].astype(BF16)
        wd_b[...] = wd_ref[0].astype(BF16)

    @pl.when(i < nvalid_ref[0])
    def _():
        x = _load_rows(xs_ref, EXPERT_TILE).astype(BF16)
        gate = jnp.minimum(_dot(x, wg_b[...]) + bg_ref[0], SWIGLU_LIMIT)
        up = jnp.clip(_dot(x, wu_b[...]) + bu_ref[0], -SWIGLU_LIMIT, SWIGLU_LIMIT)
        glu = gate * _sigmoid(SWIGLU_ALPHA * gate)
        _store_rows(ys_ref, _dot(((up + 1.0) * glu).astype(BF16), wd_b[...]) + bd_ref[0])

    @pl.when(i >= nvalid_ref[0])
    def _():
        ys_ref[...] = jnp.zeros_like(ys_ref)


def _experts(xs, tile_expert, n_valid, w_gate, b_gate, w_up, b_up, w_down, b_down):
    d, d_ff = w_gate.shape[1], w_gate.shape[2]
    tm = EXPERT_TILE
    n_tiles = xs.shape[0] // (tm * ROW_TILES)
    row_map = lambda i, te, nv: (jnp.minimum(i, nv[0] - 1), 0)
    w_map = lambda i, te, nv: (te[i], 0, 0)
    return pl.pallas_call(
        _expert_kernel,
        grid_spec=pltpu.PrefetchScalarGridSpec(
            num_scalar_prefetch=2,
            grid=(n_tiles,),
            in_specs=[pl.BlockSpec((tm * ROW_TILES, LANES), row_map),
                      pl.BlockSpec((1, d, d_ff), w_map), pl.BlockSpec((1, 1, d_ff), w_map),
                      pl.BlockSpec((1, d, d_ff), w_map), pl.BlockSpec((1, 1, d_ff), w_map),
                      pl.BlockSpec((1, d_ff, d), w_map), pl.BlockSpec((1, 1, d), w_map)],
            out_specs=pl.BlockSpec((tm * ROW_TILES, LANES), lambda i, te, nv: (i, 0)),
            scratch_shapes=[pltpu.VMEM((d, d_ff), BF16), pltpu.VMEM((d, d_ff), BF16), pltpu.VMEM((d_ff, d), BF16)],
        ),
        out_shape=jax.ShapeDtypeStruct(xs.shape, F32),
        compiler_params=pltpu.CompilerParams(dimension_semantics=("arbitrary",), vmem_limit_bytes=VMEM_LIMIT),
        name="experts",
    )(tile_expert, n_valid, xs, w_gate, b_gate[:, None, :], w_up, b_up[:, None, :], w_down, b_down[:, None, :])


def _combine_kernel(pos_ref, x1_ref, wt_ref, g_final_ref, ys_ref, yp_ref, ysmp_ref, buf_ref, sem):
    i = pl.program_id(0)
    last = pl.num_programs(0) - 1
    tb = x1_ref.shape[0]

    def row_copy(t, kk):
        return pltpu.make_async_copy(_row_tile(ys_ref, pos_ref[kk, t]), _row_tile(buf_ref.at[kk], t), sem)

    def start_rows(t, carry):
        for kk in range(TOP_K):
            row_copy(t, kk).start()
        return carry

    def wait_rows(t, carry):
        for kk in range(TOP_K):
            row_copy(t, kk).wait()
        return carry

    lax.fori_loop(0, tb, start_rows, 0)
    lax.fori_loop(0, tb, wait_rows, 0)
    out = x1_ref[...]
    for kk in range(TOP_K):
        out = out + wt_ref[:, kk:kk + 1] * _load_rows(buf_ref.at[kk], tb)
    y = _rms(out, g_final_ref[...])

    @pl.when(i < last)
    def _():
        yp_ref[...] = y

    @pl.when(i == last)
    def _():
        ysmp_ref[...] = y[0:ysmp_ref.shape[0], :]


def _combine(x1_all, pos, wt_cols, ys, g_final_row, ns):
    n, d = x1_all.shape
    tb = TOKEN_BLOCK
    nb = n // tb
    return pl.pallas_call(
        _combine_kernel,
        grid=(nb,),
        in_specs=[pl.BlockSpec((TOP_K, tb), lambda i: (0, i), memory_space=pltpu.SMEM),
                  pl.BlockSpec((tb, d), lambda i: (i, 0)),
                  pl.BlockSpec((tb, TOP_K), lambda i: (i, 0)),
                  pl.BlockSpec((1, d), lambda i: (0, 0)),
                  pl.BlockSpec(memory_space=pl.ANY)],
        out_specs=(pl.BlockSpec((tb, d), lambda i: (jnp.minimum(i, nb - 2), 0)),
                   pl.BlockSpec((ns, d), lambda i: (0, 0))),
        out_shape=(jax.ShapeDtypeStruct((n - tb, d), F32), jax.ShapeDtypeStruct((ns, d), F32)),
        scratch_shapes=[pltpu.VMEM((TOP_K, tb * ROW_TILES, LANES), F32), pltpu.SemaphoreType.DMA],
        compiler_params=pltpu.CompilerParams(dimension_semantics=("arbitrary",), vmem_limit_bytes=VMEM_LIMIT),
        name="combine",
    )(pos, x1_all, wt_cols, g_final_row, ys)


def _prepare_weights(g_mix, w_in, conv_w, conv_b, b_igate, b_fgate, g_mhnorm, w_a, w_spatial, b_spatial, g_vnorm,
                     w_b, w_out, g_ffn, w_router, b_router):
    d = w_in.shape[0]
    gate_lo = 2 * D_QK + D_VAL
    gate_hi = gate_lo + GATE_COLS
    row = lambda a: a.reshape(1, -1).astype(F32)
    w_if = w_in[:, gate_lo:gate_hi]
    gate_b = jnp.concatenate([b_igate, b_fgate])
    tril = jnp.tril(jnp.ones((SPATIAL_CHUNK, SPATIAL_CHUNK), w_spatial.dtype))
    head_of_lane = jnp.arange(D_QK) // D_K
    seg = (head_of_lane[:, None] == jnp.arange(LANES)[None, :]).astype(F32)
    pad_cols = lambda a: jnp.pad(a, ((0, 0), (0, LANES - N_HEADS)))
    return dict(
        g_mix=row(g_mix),
        w_main=jnp.concatenate([w_in[:, :gate_lo], w_in[:, gate_hi:]], axis=1).astype(BF16),
        w_ig=pad_cols(w_if[:, :N_HEADS]).astype(BF16),
        w_fg=pad_cols(w_if[:, N_HEADS:]).astype(BF16),
        w_ift=w_if.T.astype(BF16),
        b_ig=pad_cols(b_igate.reshape(1, N_HEADS)),
        b_fg=pad_cols(b_fgate.reshape(1, N_HEADS)),
        gate_b_col=jnp.broadcast_to(gate_b[:, None], (GATE_COLS, LANES)),
        conv_w=conv_w, conv_b=row(conv_b), g_mh=row(g_mhnorm), w_a=w_a.astype(BF16),
        ws_masked=(w_spatial * tril).astype(BF16),
        bsp_full=jnp.repeat(b_spatial.T, GROUP_DIM, axis=1),
        wsp_row=jnp.repeat(w_spatial[:, 0, 0], GROUP_DIM).reshape(1, -1),
        bsp_row=jnp.repeat(b_spatial[:, 0], GROUP_DIM).reshape(1, -1),
        g_vn=row(g_vnorm), w_b=w_b.astype(BF16), w_out=w_out.astype(BF16), g_ffn=row(g_ffn),
        w_rt=w_router.T, b_r_col=jnp.broadcast_to(b_router[:, None], (N_EXPERTS, LANES)),
        seg=seg, segt=seg.T,
    )


def _tile_metadata(counts, n_tiles):
    tiles_per_expert = (counts + EXPERT_TILE - 1) // EXPERT_TILE
    tile_end = jnp.cumsum(tiles_per_expert)
    n_valid = tile_end[-1]
    tile_ids = jnp.minimum(jnp.arange(n_tiles, dtype=I32), n_valid - 1)
    tile_expert = jnp.searchsorted(tile_end, tile_ids, side="right").astype(I32)
    tail_tile = jnp.where(counts > 0, tile_end - 1, -1)
    all_tiles = jnp.arange(n_tiles, dtype=I32)
    clear = jnp.logical_or(all_tiles >= n_valid, jnp.any(all_tiles[:, None] == tail_tile[None, :], axis=1))
    return tile_expert, n_valid.reshape(1).astype(I32), clear.astype(I32)


def kernel(x_prompt, x_sample, state_conv, state_C, state_n, state_m, g_mix, w_in, conv_w, conv_b, b_igate, b_fgate,
           g_mhnorm, w_a, w_spatial, b_spatial, g_vnorm, w_b, w_out, g_ffn, w_router, b_router, w_gate, b_gate,
           w_up, b_up, w_down, b_down, g_final):
    depth = g_mix.shape[0]
    assert depth == 1, "single-layer trunk"
    batch, seq, d = x_prompt.shape
    ns = x_sample.shape[0]
    assert seq % PROMPT_BLOCK == 0 and ns <= PROMPT_BLOCK and TOKEN_BLOCK == PROMPT_BLOCK
    wts = _prepare_weights(g_mix[0], w_in[0], conv_w[0], conv_b[0], b_igate[0], b_fgate[0], g_mhnorm[0], w_a[0],
                           w_spatial[0], b_spatial[0], g_vnorm[0], w_b[0], w_out[0], g_ffn[0], w_router[0],
                           b_router[0])

    x1_s, h2_s, s_conv, s_c, s_n, s_m, vg = _sample_mixer(
        x_sample, state_conv[0], state_C[0], state_n[0], state_m[0], wts)
    x1_all, h2_all, p_conv, p_c, p_n, p_m = _prompt_mixer(x_prompt, x1_s, h2_s, wts)
    n_total = x1_all.shape[0]

    pos, wt, counts = _router(h2_all, wts)
    n_tiles = -(-(n_total * TOP_K) // EXPERT_TILE) + N_EXPERTS
    tile_expert, n_valid, clear = _tile_metadata(counts[:, 0].astype(I32), n_tiles)
    xs = _dispatch(h2_all, pos, clear, n_tiles * EXPERT_TILE)
    ys = _experts(xs, tile_expert, n_valid, w_gate[0], b_gate[0], w_up[0], b_up[0], w_down[0], b_down[0])
    y_prompt, y_sample = _combine(x1_all, pos, wt.T, ys, g_final.reshape(1, d), ns)

    return (y_prompt.reshape(batch, seq, d), y_sample.reshape(ns, 1, d),
            p_conv[None], p_c[None], p_n[None], p_m[None, :, :, 0],
            s_conv[None], s_c[None], s_n[None], s_m[None], vg.reshape(1, ns, 1, N_GROUPS, GROUP_DIM))
```

```python
import functools

import jax
import jax.numpy as jnp
from jax import lax
from jax.experimental import pallas as pl
from jax.experimental.pallas import tpu as pltpu

F32 = jnp.float32
BF16 = jnp.bfloat16
I32 = jnp.int32
HIGHEST = lax.Precision.HIGHEST

N_HEADS = 8
D_K = 64
D_V = 128
D_QK = N_HEADS * D_K
D_VAL = N_HEADS * D_V
CONV_TAPS = 4
GATE_COLS = 2 * N_HEADS
N_GROUPS = 8
GROUP_DIM = 128
SPATIAL_CHUNK = 128
N_EXPERTS = 32
TOP_K = 4
SWIGLU_LIMIT = 7.0
SWIGLU_ALPHA = 1.702
RMS_EPS = 1e-6
EMPTY_MAX = -1e30
LANES = 128
SUBLANES = 8
D_MODEL = 1024
ROW_TILES = D_MODEL // LANES

PROMPT_BLOCK = 256
SAMPLE_STATE_BLOCK = 8
TOKEN_BLOCK = PROMPT_BLOCK
EXPERT_TILE = 512
VMEM_LIMIT = 56 * 1024 * 1024


def _dot(a, b, precision=None):
    return jnp.dot(a, b, preferred_element_type=F32, precision=precision)


def _dot_nt(a, b, precision=None):
    return lax.dot_general(a, b, (((1,), (1,)), ((), ())), preferred_element_type=F32, precision=precision)


def _dot_tn(a, b, precision=None):
    return lax.dot_general(a, b, (((0,), (0,)), ((), ())), preferred_element_type=F32, precision=precision)


def _rms(x, g):
    return x * lax.rsqrt(jnp.mean(x * x, axis=-1, keepdims=True) + RMS_EPS) * g


def _sigmoid(x):
    return jax.nn.sigmoid(x)


def _gelu(x):
    return 0.5 * x * (1.0 + lax.erf(x * (2.0 ** -0.5)))


def _log_sigmoid(x):
    return jax.nn.log_sigmoid(x)


def _store_rows(ref, val):
    rows = val.shape[0]
    for jt in range(ROW_TILES):
        ref[pl.ds(jt, rows, stride=ROW_TILES), :] = val[:, jt * LANES:(jt + 1) * LANES]


def _load_rows(ref, rows):
    return jnp.concatenate([ref[pl.ds(jt, rows, stride=ROW_TILES), :] for jt in range(ROW_TILES)], axis=1)


def _row_tile(ref, r):
    return ref.at[pl.ds(pl.multiple_of(r * ROW_TILES, ROW_TILES), ROW_TILES)]


def _in_projection(hn, w_main_ref, piece):
    return _dot(hn, w_main_ref[:, piece * 1024:(piece + 1) * 1024])


def _post_mixer(x, hm, z_o, z_u, z_vg, z_ga, z_gb, spatial_fn,
                g_mh_ref, w_a_ref, g_vn_ref, w_b_ref, w_out_ref, g_ffn_ref):
    parts = []
    for h in range(N_HEADS):
        hh = hm[:, h * D_V:(h + 1) * D_V]
        parts.append(hh * lax.rsqrt(jnp.mean(hh * hh, axis=-1, keepdims=True) + RMS_EPS))
    hmn = jnp.concatenate(parts, axis=1) * g_mh_ref[...]
    y_a = _dot((hmn * _sigmoid(z_o)).astype(BF16), w_a_ref[...])
    u = _gelu(z_u)
    vg = _rms(_gelu(z_vg), g_vn_ref[...])
    s = spatial_fn(vg)
    y_b = _dot((u * s).astype(BF16), w_b_ref[...])
    merged = _sigmoid(z_ga) * y_a + _sigmoid(z_gb) * y_b
    x1 = x + _dot(merged.astype(BF16), w_out_ref[...])
    h2 = _rms(x1, g_ffn_ref[...])
    return x1, h2, vg


def _prompt_mixer_kernel(x_ref, x1s_ref, h2s_ref, *refs, blocks_per_seq, n_prompt_blocks):
    g = pl.program_id(0)
    x1_ref, h2_ref = refs[18], refs[19]

    @pl.when(g < n_prompt_blocks)
    def _():
        _prompt_block(g % blocks_per_seq, x_ref, *refs)

    @pl.when(g == n_prompt_blocks)
    def _():
        ns = x1s_ref.shape[0]
        x1_ref[0:ns, :] = x1s_ref[...]
        x1_ref[ns:, :] = jnp.zeros((x1_ref.shape[0] - ns, x1_ref.shape[1]), F32)
        h2_ref[0:ns * ROW_TILES, :] = h2s_ref[...]
        h2_ref[ns * ROW_TILES:, :] = jnp.zeros((h2_ref.shape[0] - ns * ROW_TILES, LANES), F32)


def _prompt_block(j, x_ref, g_mix_ref, w_main_ref, w_ig_ref, w_fg_ref, w_ift_ref, b_ig_ref, b_fg_ref,
                  gate_b_col_ref, conv_w_ref, conv_b_ref, g_mh_ref, w_a_ref, ws_ref, bsp_ref, g_vn_ref,
                  w_b_ref, w_out_ref, g_ffn_ref,
                  x1_ref, h2_ref, pconv_ref, pc_ref, pn_ref, pm_ref,
                  xp_ref):
    tb = x_ref.shape[1]

    @pl.when(j == 0)
    def _():
        pc_ref[...] = jnp.zeros_like(pc_ref)
        pn_ref[...] = jnp.zeros_like(pn_ref)
        pm_ref[...] = jnp.full(pm_ref.shape, EMPTY_MAX, F32)
        xp_ref[0:SUBLANES, :] = jnp.zeros((SUBLANES, xp_ref.shape[1]), F32)

    x = x_ref[0]
    hn = _rms(x, g_mix_ref[...]).astype(BF16)

    z_qk = _in_projection(hn, w_main_ref, 0)
    xp_ref[SUBLANES:SUBLANES + tb, :] = z_qk
    y = conv_b_ref[...] + conv_w_ref[CONV_TAPS - 1:CONV_TAPS, :] * z_qk
    for tap in range(CONV_TAPS - 1):
        back = CONV_TAPS - 1 - tap
        y = y + conv_w_ref[tap:tap + 1, :] * xp_ref[SUBLANES - back:SUBLANES - back + tb, :]
    tail = z_qk[tb - (CONV_TAPS - 1):tb, :]
    xp_ref[SUBLANES - (CONV_TAPS - 1):SUBLANES, :] = tail
    pconv_ref[0] = tail
    qk = y * _sigmoid(y)
    q = qk[:, :D_QK]
    k = qk[:, D_QK:] * (D_K ** -0.5)
    qb = q.astype(BF16)
    v = _in_projection(hn, w_main_ref, 1)
    vb = v.astype(BF16)

    ig_c = _dot(hn, w_ig_ref[...]) + b_ig_ref[...]
    lf_c = _log_sigmoid(_dot(hn, w_fg_ref[...]) + b_fg_ref[...])
    gr = _dot_nt(w_ift_ref[...], hn) + gate_b_col_ref[:, 0:1]
    ig_r = gr[0:N_HEADS, :]
    lf_r = _log_sigmoid(gr[N_HEADS:GATE_COLS, :])
    row_i = lax.broadcasted_iota(I32, (tb, tb), 0)
    col_i = lax.broadcasted_iota(I32, (tb, tb), 1)
    causal = col_i <= row_i
    tri = causal.astype(F32)
    b_c = _dot(tri, lf_c, HIGHEST)
    b_r = _dot_nt(lf_r, tri, HIGHEST)

    hm_parts = []
    for h in range(N_HEADS):
        bt = b_c[:, h:h + 1]
        ig_col = ig_c[:, h:h + 1]
        bs = b_r[h:h + 1, :]
        ig_row = ig_r[h:h + 1, :]
        m_prev = pm_ref[0, h:h + 1, 0:1]
        c_prev = pc_ref[0, h]
        n_prev = pn_ref[0, h:h + 1, :]
        q_h = q[:, h * D_K:(h + 1) * D_K]
        qb_h = qb[:, h * D_K:(h + 1) * D_K]
        k_h = k[:, h * D_K:(h + 1) * D_K]
        vb_h = vb[:, h * D_V:(h + 1) * D_V]

        dmat = jnp.where(causal, bt - bs + ig_row, -jnp.inf)
        inter = bt + m_prev
        m_t = jnp.maximum(inter, jnp.max(dmat, axis=-1, keepdims=True))
        w_inter = jnp.exp(inter - m_t)
        s = _dot_nt(qb_h, k_h.astype(BF16)) * jnp.exp(dmat - m_t)
        num = w_inter * _dot(qb_h, c_prev.astype(BF16)) + _dot(s.astype(BF16), vb_h)
        den = w_inter * jnp.sum(q_h * n_prev, axis=-1, keepdims=True) + jnp.sum(s, axis=-1, keepdims=True)
        hm_parts.append(num / jnp.maximum(jnp.abs(den), jnp.exp(-m_t)))

        m_new = m_t[tb - 1:tb, :]
        b_last = bt[tb - 1:tb, :]
        decay = jnp.exp(b_last + m_prev - m_new)
        w_s = jnp.exp(b_last - bt + ig_col - m_new)
        kw = k_h * w_s
        pc_ref[0, h] = decay * c_prev + _dot_tn(kw.astype(BF16), vb_h)
        pn_ref[0, h:h + 1, :] = decay * n_prev + jnp.sum(kw, axis=0, keepdims=True)
        pm_ref[0, h:h + 1, :] = jnp.broadcast_to(m_new, (1, pm_ref.shape[2]))
    hm = jnp.concatenate(hm_parts, axis=1)

    def spatial(vg):
        vgb = vg.astype(BF16)
        rows = []
        for c in range(tb // SPATIAL_CHUNK):
            blocks = []
            for g in range(N_GROUPS):
                blk = vgb[c * SPATIAL_CHUNK:(c + 1) * SPATIAL_CHUNK, g * GROUP_DIM:(g + 1) * GROUP_DIM]
                blocks.append(_dot(ws_ref[g], blk))
            rows.append(jnp.concatenate(blocks, axis=1) + bsp_ref[...])
        return jnp.concatenate(rows, axis=0)

    z_o = _in_projection(hn, w_main_ref, 2)
    z_u = _in_projection(hn, w_main_ref, 3)
    z_vg = _in_projection(hn, w_main_ref, 4)
    z_ga = _in_projection(hn, w_main_ref, 5)
    z_gb = _in_projection(hn, w_main_ref, 6)
    x1, h2, _ = _post_mixer(x, hm, z_o, z_u, z_vg, z_ga, z_gb, spatial,
                            g_mh_ref, w_a_ref, g_vn_ref, w_b_ref, w_out_ref, g_ffn_ref)
    x1_ref[...] = x1
    _store_rows(h2_ref, h2)


def _whole_vmem():
    return pl.BlockSpec(memory_space=pltpu.VMEM)


def _prompt_mixer(x_prompt, x1_sample, h2_sample, wts):
    batch, seq, d = x_prompt.shape
    tb = PROMPT_BLOCK
    nt = seq // tb
    n_blocks = batch * nt
    n_total = (n_blocks + 1) * tb
    out_shapes = (
        jax.ShapeDtypeStruct((n_total, d), F32),
        jax.ShapeDtypeStruct((n_total * ROW_TILES, LANES), F32),
        jax.ShapeDtypeStruct((batch, CONV_TAPS - 1, 2 * D_QK), F32),
        jax.ShapeDtypeStruct((batch, N_HEADS, D_K, D_V), F32),
        jax.ShapeDtypeStruct((batch, N_HEADS, D_K), F32),
        jax.ShapeDtypeStruct((batch, N_HEADS, LANES), F32),
    )
    seq_of = lambda g: jnp.minimum(g, n_blocks - 1) // nt
    row_map = lambda g: (g, 0)
    out_specs = (
        pl.BlockSpec((tb, d), row_map),
        pl.BlockSpec((tb * ROW_TILES, LANES), row_map),
        pl.BlockSpec((1, CONV_TAPS - 1, 2 * D_QK), lambda g: (seq_of(g), 0, 0)),
        pl.BlockSpec((1, N_HEADS, D_K, D_V), lambda g: (seq_of(g), 0, 0, 0)),
        pl.BlockSpec((1, N_HEADS, D_K), lambda g: (seq_of(g), 0, 0)),
        pl.BlockSpec((1, N_HEADS, LANES), lambda g: (seq_of(g), 0, 0)),
    )
    in_specs = ([pl.BlockSpec((1, tb, d), lambda g: (seq_of(g), jnp.minimum(g, n_blocks - 1) % nt, 0))]
                + [_whole_vmem()] * 20)
    return pl.pallas_call(
        functools.partial(_prompt_mixer_kernel, blocks_per_seq=nt, n_prompt_blocks=n_blocks),
        grid=(n_blocks + 1,),
        in_specs=in_specs,
        out_specs=out_specs,
        out_shape=out_shapes,
        scratch_shapes=[pltpu.VMEM((SUBLANES + tb, 2 * D_QK), F32)],
        compiler_params=pltpu.CompilerParams(dimension_semantics=("arbitrary",), vmem_limit_bytes=VMEM_LIMIT),
        name="prompt_mixer",
    )(x_prompt, x1_sample, h2_sample,
      wts["g_mix"], wts["w_main"], wts["w_ig"], wts["w_fg"], wts["w_ift"], wts["b_ig"], wts["b_fg"],
      wts["gate_b_col"], wts["conv_w"], wts["conv_b"], wts["g_mh"], wts["w_a"], wts["ws_masked"], wts["bsp_full"],
      wts["g_vn"], wts["w_b"], wts["w_out"], wts["g_ffn"])


def _sample_pre_kernel(x_ref, c0_ref, c1_ref, c2_ref, n_ref, m_ref,
                       g_mix_ref, w_main_ref, w_ig_ref, w_fg_ref, b_ig_ref, b_fg_ref, conv_w_ref, conv_b_ref,
                       seg_ref, segt_ref,
                       qw_ref, kw_ref, v_ref, scal_ref, nnew_ref, zqk_ref, rest_ref):
    x = x_ref[...]
    hn = _rms(x, g_mix_ref[...]).astype(BF16)
    z_qk = _in_projection(hn, w_main_ref, 0)
    zqk_ref[...] = z_qk
    y = (conv_b_ref[...] + conv_w_ref[0:1, :] * c0_ref[...] + conv_w_ref[1:2, :] * c1_ref[...]
         + conv_w_ref[2:3, :] * c2_ref[...] + conv_w_ref[3:4, :] * z_qk)
    qk = y * _sigmoid(y)
    q = qk[:, :D_QK]
    k = qk[:, D_QK:] * (D_K ** -0.5)
    v_ref[...] = _in_projection(hn, w_main_ref, 1)
    for p in range(5):
        rest_ref[:, p * 1024:(p + 1) * 1024] = _in_projection(hn, w_main_ref, 2 + p)

    ig = _dot(hn, w_ig_ref[...]) + b_ig_ref[...]
    lf = _log_sigmoid(_dot(hn, w_fg_ref[...]) + b_fg_ref[...])
    m_prev = m_ref[...]
    m_new = jnp.maximum(lf + m_prev, ig)
    decay = jnp.exp(lf + m_prev - m_new)
    w_s = jnp.exp(ig - m_new)
    n_prev = n_ref[...]
    seg = seg_ref[...]
    segt = segt_ref[...]
    qk_dot = _dot(q * k, seg, HIGHEST)
    qn_dot = _dot(q * n_prev, seg, HIGHEST)
    s = qk_dot * w_s
    den = decay * qn_dot + s
    denom = jnp.maximum(jnp.abs(den), jnp.exp(-m_new))
    decay_x = _dot(decay, segt, HIGHEST)
    ws_x = _dot(w_s, segt, HIGHEST)
    qw_ref[...] = q * decay_x
    kw = k * ws_x
    kw_ref[...] = kw
    nnew_ref[...] = decay_x * n_prev + kw
    scal_ref[0] = decay
    scal_ref[1] = s
    scal_ref[2] = denom
    scal_ref[3] = m_new


def _sample_state_kernel(c_ref, qw_ref, kw_ref, v_ref, dec_ref, s_ref, den_ref, cnew_ref, h_ref):
    c = c_ref[...]
    v = v_ref[...]
    cnew_ref[...] = dec_ref[...] * c + kw_ref[...] * v
    num = jnp.sum(qw_ref[...] * c, axis=2, keepdims=True) + s_ref[...] * v
    h_ref[...] = num / den_ref[...]


def _sample_post_kernel(x_ref, hm_ref, rest_ref,
                        wsp_ref, bsp_ref, g_mh_ref, w_a_ref, g_vn_ref, w_b_ref, w_out_ref, g_ffn_ref,
                        x1_ref, h2_ref, vg_ref):
    z =[rest_ref[:, p * 1024:(p + 1) * 1024] for p in range(5)]
    spatial = lambda vg: vg * wsp_ref[...] + bsp_ref[...]
    x1, h2, vg = _post_mixer(x_ref[...], hm_ref[...], z[0], z[1], z[2], z[3], z[4], spatial,
                             g_mh_ref, w_a_ref, g_vn_ref, w_b_ref, w_out_ref, g_ffn_ref)
    x1_ref[...] = x1
    _store_rows(h2_ref, h2)
    vg_ref[...] = vg


def _sample_mixer(x_sample, state_conv, state_c, state_n, state_m, wts):
    ns, _, d = x_sample.shape
    xs = x_sample.reshape(ns, d)
    c_taps = [state_conv[:, t, :] for t in range(CONV_TAPS - 1)]
    n_rows = state_n.reshape(ns, D_QK)
    pre_out = (
        jax.ShapeDtypeStruct((ns, D_QK), F32),
        jax.ShapeDtypeStruct((ns, D_QK), F32),
        jax.ShapeDtypeStruct((ns, D_VAL), F32),
        jax.ShapeDtypeStruct((4, ns, LANES), F32),
        jax.ShapeDtypeStruct((ns, D_QK), F32),
        jax.ShapeDtypeStruct((ns, 2 * D_QK), F32),
        jax.ShapeDtypeStruct((ns, 5 * 1024), F32),
    )
    qw, kw, v, scal, n_new, z_qk, rest = pl.pallas_call(
        _sample_pre_kernel,
        out_shape=pre_out,
        in_specs=[_whole_vmem()] * 16,
        out_specs=tuple(_whole_vmem() for _ in pre_out),
        compiler_params=pltpu.CompilerParams(vmem_limit_bytes=VMEM_LIMIT),
        name="sample_pre",
    )(xs, c_taps[0], c_taps[1], c_taps[2], n_rows, jnp.pad(state_m, ((0, 0), (0, LANES - N_HEADS))),
      wts["g_mix"], wts["w_main"], wts["w_ig"], wts["w_fg"], wts["b_ig"], wts["b_fg"], wts["conv_w"],
      wts["conv_b"], wts["seg"], wts["segt"])
    scal = scal[:, :, :N_HEADS]

    bb = SAMPLE_STATE_BLOCK
    col4 = lambda a: a.reshape(ns, N_HEADS, D_K, 1)
    sc4 = lambda a: a.reshape(ns, N_HEADS, 1, 1)
    blk = lambda *tail: pl.BlockSpec((bb,) + tail, lambda i: (i, 0, 0, 0))
    c_new, hm = pl.pallas_call(
        _sample_state_kernel,
        grid=(ns // bb,),
        in_specs=[blk(N_HEADS, D_K, D_V), blk(N_HEADS, D_K, 1), blk(N_HEADS, D_K, 1), blk(N_HEADS, 1, D_V),
                  blk(N_HEADS, 1, 1), blk(N_HEADS, 1, 1), blk(N_HEADS, 1, 1)],
        out_specs=(blk(N_HEADS, D_K, D_V), blk(N_HEADS, 1, D_V)),
        out_shape=(jax.ShapeDtypeStruct((ns, N_HEADS, D_K, D_V), F32),
                   jax.ShapeDtypeStruct((ns, N_HEADS, 1, D_V), F32)),
        compiler_params=pltpu.CompilerParams(dimension_semantics=("arbitrary",), vmem_limit_bytes=VMEM_LIMIT),
        name="sample_state",
    )(state_c, col4(qw), col4(kw), v.reshape(ns, N_HEADS, 1, D_V), sc4(scal[0]), sc4(scal[1]), sc4(scal[2]))

    post_out = (jax.ShapeDtypeStruct((ns, d), F32), jax.ShapeDtypeStruct((ns * ROW_TILES, LANES), F32),
                jax.ShapeDtypeStruct((ns, d), F32))
    x1_s, h2_s, vg = pl.pallas_call(
        _sample_post_kernel,
        out_shape=post_out,
        in_specs=[_whole_vmem()] * 11,
        out_specs=tuple(_whole_vmem() for _ in post_out),
        compiler_params=pltpu.CompilerParams(vmem_limit_bytes=VMEM_LIMIT),
        name="sample_post",
    )(xs, hm.reshape(ns, D_VAL), rest, wts["wsp_row"], wts["bsp_row"], wts["g_mh"], wts["w_a"], wts["g_vn"],
      wts["w_b"], wts["w_out"], wts["g_ffn"])
    s_conv = jnp.concatenate([state_conv[:, 1:, :], z_qk[:, None, :]], axis=1)
    return x1_s, h2_s, s_conv, c_new, n_new.reshape(ns, N_HEADS, D_K), scal[3], vg


def _router_kernel(h2_ref, w_rt_ref, b_r_ref, pos_ref, wt_ref, cnt_ref, run_ref, off_ref):
    phase = pl.program_id(0)
    i = pl.program_id(1)
    tb = h2_ref.shape[0] // ROW_TILES

    @pl.when(jnp.logical_and(phase == 0, i == 0))
    def _():
        cnt_ref[...] = jnp.zeros_like(cnt_ref)

    logits = _dot_nt(w_rt_ref[...], _load_rows(h2_ref, tb), HIGHEST) + b_r_ref[:, 0:1]
    e_iota = lax.broadcasted_iota(I32, logits.shape, 0)
    sels, vals = [], []
    work = logits
    for _ in range(TOP_K):
        mx = jnp.max(work, axis=0, keepdims=True)
        idx = jnp.min(jnp.where(work == mx, e_iota, N_EXPERTS), axis=0, keepdims=True)
        sel = e_iota == idx
        work = jnp.where(sel, -jnp.inf, work)
        sels.append(sel.astype(F32))
        vals.append(mx)
    sel_all = sels[0] + sels[1] + sels[2] + sels[3]
    block_count = jnp.sum(sel_all, axis=1, keepdims=True)

    @pl.when(phase == 0)
    def _():
        cnt_ref[...] += jnp.broadcast_to(block_count, cnt_ref.shape)

    @pl.when(jnp.logical_and(phase == 1, i == 0))
    def _():
        cnt = cnt_ref[...]
        padded = jnp.ceil(cnt * (1.0 / EXPERT_TILE)) * EXPERT_TILE
        r = lax.broadcasted_iota(I32, (N_EXPERTS, N_EXPERTS), 0)
        c = lax.broadcasted_iota(I32, (N_EXPERTS, N_EXPERTS), 1)
        off_ref[...] = _dot((c < r).astype(F32), padded, HIGHEST)
        run_ref[...] = jnp.zeros_like(run_ref)

    @pl.when(phase == 1)
    def _():
        r = lax.broadcasted_iota(I32, (tb, tb), 0)
        c = lax.broadcasted_iota(I32, (tb, tb), 1)
        before = _dot(sel_all.astype(BF16), (r < c).astype(BF16))
        row = before + run_ref[:, 0:1] + off_ref[:, 0:1]
        es = [jnp.exp(vv - vals[0]) for vv in vals]
        tot = es[0] + es[1] + es[2] + es[3]
        for kk in range(TOP_K):
            pos_ref[kk:kk + 1, :] = jnp.sum(sels[kk] * row, axis=0, keepdims=True).astype(I32)
            wt_ref[kk:kk + 1, :] = es[kk] / tot
        run_ref[...] += jnp.broadcast_to(block_count, run_ref.shape)


def _router(h2_all, wts):
    n = h2_all.shape[0] // ROW_TILES
    tb = TOKEN_BLOCK
    nb = n // tb
    return pl.pallas_call(
        _router_kernel,
        grid=(2, nb),
        in_specs=[pl.BlockSpec((tb * ROW_TILES, LANES), lambda p, i: (i, 0)), _whole_vmem(), _whole_vmem()],
        out_specs=(pl.BlockSpec((TOP_K, tb), lambda p, i: (0, i * p)),
                   pl.BlockSpec((TOP_K, tb), lambda p, i: (0, i * p)),
                   pl.BlockSpec((N_EXPERTS, LANES), lambda p, i: (0, 0))),
        out_shape=(jax.ShapeDtypeStruct((TOP_K, n), I32), jax.ShapeDtypeStruct((TOP_K, n), F32),
                   jax.ShapeDtypeStruct((N_EXPERTS, LANES), F32)),
        scratch_shapes=[pltpu.VMEM((N_EXPERTS, LANES), F32), pltpu.VMEM((N_EXPERTS, LANES), F32)],
        compiler_params=pltpu.CompilerParams(
            dimension_semantics=("arbitrary", "arbitrary"), vmem_limit_bytes=VMEM_LIMIT),
        name="router",
    )(h2_all, wts["w_rt"], wts["b_r_col"])


def _dispatch_kernel(clear_ref, pos_ref, h2_ref, xs_ref, zero_ref, sem, zsem):
    i = pl.program_id(0)
    tb = pos_ref.shape[1]
    tile_rows = EXPERT_TILE * ROW_TILES
    n_tiles = xs_ref.shape[0] // tile_rows

    @pl.when(i == 0)
    def _():
        zero_ref[...] = jnp.zeros_like(zero_ref)

        def tile_copy(t):
            start_row = pl.multiple_of(t * tile_rows, tile_rows)
            return pltpu.make_async_copy(zero_ref, xs_ref.at[pl.ds(start_row, tile_rows)], zsem)

        def start(t, carry):
            @pl.when(clear_ref[t] > 0)
            def _():
                tile_copy(t).start()
            return carry

        def wait(t, carry):
            @pl.when(clear_ref[t] > 0)
            def _():
                tile_copy(t).wait()
            return carry

        lax.fori_loop(0, n_tiles, start, 0)
        lax.fori_loop(0, n_tiles, wait, 0)

    def row_copy(t, kk):
        return pltpu.make_async_copy(_row_tile(h2_ref, t), _row_tile(xs_ref, pos_ref[kk, t]), sem)

    def start_rows(t, carry):
        for kk in range(TOP_K):
            row_copy(t, kk).start()
        return carry

    def wait_rows(t, carry):
        for kk in range(TOP_K):
            row_copy(t, kk).wait()
        return carry

    lax.fori_loop(0, tb, start_rows, 0)
    lax.fori_loop(0, tb, wait_rows, 0)


def _dispatch(h2_all, pos, tail_tile, n_rows):
    n = h2_all.shape[0] // ROW_TILES
    tb = TOKEN_BLOCK
    return pl.pallas_call(
        _dispatch_kernel,
        grid_spec=pltpu.PrefetchScalarGridSpec(
            num_scalar_prefetch=1,
            grid=(n // tb,),
            in_specs=[pl.BlockSpec((TOP_K, tb), lambda i, tail: (0, i), memory_space=pltpu.SMEM),
                      pl.BlockSpec((tb * ROW_TILES, LANES), lambda i, tail: (i, 0))],
            out_specs=pl.BlockSpec(memory_space=pl.ANY),
            scratch_shapes=[pltpu.VMEM((EXPERT_TILE * ROW_TILES, LANES), F32), pltpu.SemaphoreType.DMA,
                            pltpu.SemaphoreType.DMA],
        ),
        out_shape=jax.ShapeDtypeStruct((n_rows * ROW_TILES, LANES), F32),
        compiler_params=pltpu.CompilerParams(dimension_semantics=("arbitrary",), vmem_limit_bytes=VMEM_LIMIT),
        name="dispatch",
    )(tail_tile, pos, h2_all)


def _expert_kernel(tile_e_ref, nvalid_ref, xs_ref, wg_ref, bg_ref, wu_ref, bu_ref, wd_ref, bd_ref, ys_ref,
                   wg_b, wu_b, wd_b):
    i = pl.program_id(0)
    prev = tile_e_ref[jnp.maximum(i - 1, 0)]
    fresh = jnp.logical_or(i == 0, tile_e_ref[i] != prev)

    @pl.when(jnp.logical_and(i < nvalid_ref[0], fresh))
    def _():
        wg_b[...] = wg_ref[0].astype(BF16)
        wu_b[...] = wu_ref[0].astype(BF16)
        wd_b[...] = wd_ref[0].astype(BF16)

    @pl.when(i < nvalid_ref[0])
    def _():
        x = _load_rows(xs_ref, EXPERT_TILE).astype(BF16)
        gate = jnp.minimum(_dot(x, wg_b[...]) + bg_ref[0], SWIGLU_LIMIT)
        up = jnp.clip(_dot(x, wu_b[...]) + bu_ref[0], -SWIGLU_LIMIT, SWIGLU_LIMIT)
        glu = gate * _sigmoid(SWIGLU_ALPHA * gate)
        _store_rows(ys_ref, _dot(((up + 1.0) * glu).astype(BF16), wd_b[...]) + bd_ref[0])

    @pl.when(i >= nvalid_ref[0])
    def _():
        ys_ref[...] = jnp.zeros_like(ys_ref)


def _experts(xs, tile_expert, n_valid, w_gate, b_gate, w_up, b_up, w_down, b_down):
    d, d_ff = w_gate.shape[1], w_gate.shape[2]
    tm = EXPERT_TILE
    n_tiles = xs.shape[0] // (tm * ROW_TILES)
    row_map = lambda i, te, nv: (jnp.minimum(i, nv[0] - 1), 0)
    w_map = lambda i, te, nv: (te[i], 0, 0)
    return pl.pallas_call(
        _expert_kernel,
        grid_spec=pltpu.PrefetchScalarGridSpec(
            num_scalar_prefetch=2,
            grid=(n_tiles,),
            in_specs=[pl.BlockSpec((tm * ROW_TILES, LANES), row_map),
                      pl.BlockSpec((1, d, d_ff), w_map), pl.BlockSpec((1, 1, d_ff), w_map),
                      pl.BlockSpec((1, d, d_ff), w_map), pl.BlockSpec((1, 1, d_ff), w_map),
                      pl.BlockSpec((1, d_ff, d), w_map), pl.BlockSpec((1, 1, d), w_map)],
            out_specs=pl.BlockSpec((tm * ROW_TILES, LANES), lambda i, te, nv: (i, 0)),
            scratch_shapes=[pltpu.VMEM((d, d_ff), BF16), pltpu.VMEM((d, d_ff), BF16), pltpu.VMEM((d_ff, d), BF16)],
        ),
        out_shape=jax.ShapeDtypeStruct(xs.shape, F32),
        compiler_params=pltpu.CompilerParams(dimension_semantics=("arbitrary",), vmem_limit_bytes=VMEM_LIMIT),
        name="experts",
    )(tile_expert, n_valid, xs, w_gate, b_gate[:, None, :], w_up, b_up[:, None, :], w_down, b_down[:, None, :])


def _combine_kernel(pos_ref, x1_ref, wt_ref, g_final_ref, ys_ref, yp_ref, ysmp_ref, buf_ref, sem):
    i = pl.program_id(0)
    last = pl.num_programs(0) - 1
    tb = x1_ref.shape[0]

    def row_copy(t, kk):
        return pltpu.make_async_copy(_row_tile(ys_ref, pos_ref[kk, t]), _row_tile(buf_ref.at[kk], t), sem)

    def start_rows(t, carry):
        for kk in range(TOP_K):
            row_copy(t, kk).start()
        return carry

    def wait_rows(t, carry):
        for kk in range(TOP_K):
            row_copy(t, kk).wait()
        return carry

    lax.fori_loop(0, tb, start_rows, 0)
    lax.fori_loop(0, tb, wait_rows, 0)
    out = x1_ref[...]
    for kk in range(TOP_K):
        out = out + wt_ref[:, kk:kk + 1] * _load_rows(buf_ref.at[kk], tb)
    y = _rms(out, g_final_ref[...])

    @pl.when(i < last)
    def _():
        yp_ref[...] = y

    @pl.when(i == last)
    def _():
        ysmp_ref[...] = y[0:ysmp_ref.shape[0], :]


def _combine(x1_all, pos, wt_cols, ys, g_final_row, ns):
    n, d = x1_all.shape
    tb = TOKEN_BLOCK
    nb = n // tb
    return pl.pallas_call(
        _combine_kernel,
        grid=(nb,),
        in_specs=[pl.BlockSpec((TOP_K, tb), lambda i: (0, i), memory_space=pltpu.SMEM),
                  pl.BlockSpec((tb, d), lambda i: (i, 0)),
                  pl.BlockSpec((tb, TOP_K), lambda i: (i, 0)),
                  pl.BlockSpec((1, d), lambda i: (0, 0)),
                  pl.BlockSpec(memory_space=pl.ANY)],
        out_specs=(pl.BlockSpec((tb, d), lambda i: (jnp.minimum(i, nb - 2), 0)),
                   pl.BlockSpec((ns, d), lambda i: (0, 0))),
        out_shape=(jax.ShapeDtypeStruct((n - tb, d), F32), jax.ShapeDtypeStruct((ns, d), F32)),
        scratch_shapes=[pltpu.VMEM((TOP_K, tb * ROW_TILES, LANES), F32), pltpu.SemaphoreType.DMA],
        compiler_params=pltpu.CompilerParams(dimension_semantics=("arbitrary",), vmem_limit_bytes=VMEM_LIMIT),
        name="combine",
    )(pos, x1_all, wt_cols, g_final_row, ys)


def _prepare_weights(g_mix, w_in, conv_w, conv_b, b_igate, b_fgate, g_mhnorm, w_a, w_spatial, b_spatial, g_vnorm,
                     w_b, w_out, g_ffn, w_router, b_router):
    d = w_in.shape[0]
    gate_lo = 2 * D_QK + D_VAL
    gate_hi = gate_lo + GATE_COLS
    row = lambda a: a.reshape(1, -1).astype(F32)
    w_if = w_in[:, gate_lo:gate_hi]
    gate_b = jnp.concatenate([b_igate, b_fgate])
    tril = jnp.tril(jnp.ones((SPATIAL_CHUNK, SPATIAL_CHUNK), w_spatial.dtype))
    head_of_lane = jnp.arange(D_QK) // D_K
    seg = (head_of_lane[:, None] == jnp.arange(LANES)[None, :]).astype(F32)
    pad_cols = lambda a: jnp.pad(a, ((0, 0), (0, LANES - N_HEADS)))
    return dict(
        g_mix=row(g_mix),
        w_main=jnp.concatenate([w_in[:, :gate_lo], w_in[:, gate_hi:]], axis=1).astype(BF16),
        w_ig=pad_cols(w_if[:, :N_HEADS]).astype(BF16),
        w_fg=pad_cols(w_if[:, N_HEADS:]).astype(BF16),
        w_ift=w_if.T.astype(BF16),
        b_ig=pad_cols(b_igate.reshape(1, N_HEADS)),
        b_fg=pad_cols(b_fgate.reshape(1, N_HEADS)),
        gate_b_col=jnp.broadcast_to(gate_b[:, None], (GATE_COLS, LANES)),
        conv_w=conv_w, conv_b=row(conv_b), g_mh=row(g_mhnorm), w_a=w_a.astype(BF16),
        ws_masked=(w_spatial * tril).astype(BF16),
        bsp_full=jnp.repeat(b_spatial.T, GROUP_DIM, axis=1),
        wsp_row=jnp.repeat(w_spatial[:, 0, 0], GROUP_DIM).reshape(1, -1),
        bsp_row=jnp.repeat(b_spatial[:, 0], GROUP_DIM).reshape(1, -1),
        g_vn=row(g_vnorm), w_b=w_b.astype(BF16), w_out=w_out.astype(BF16), g_ffn=row(g_ffn),
        w_rt=w_router.T, b_r_col=jnp.broadcast_to(b_router[:, None], (N_EXPERTS, LANES)),
        seg=seg, segt=seg.T,
    )


def _tile_metadata(counts, n_tiles):
    tiles_per_expert = (counts + EXPERT_TILE - 1) // EXPERT_TILE
    tile_end = jnp.cumsum(tiles_per_expert)
    n_valid = tile_end[-1]
    tile_ids = jnp.minimum(jnp.arange(n_tiles, dtype=I32), n_valid - 1)
    tile_expert = jnp.sum(tile_end[None, :] <= tile_ids[:, None], axis=1).astype(I32)
    tail_tile = jnp.where(counts > 0, tile_end - 1, -1)
    all_tiles = jnp.arange(n_tiles, dtype=I32)
    clear = jnp.logical_or(all_tiles >= n_valid, jnp.any(all_tiles[:, None] == tail_tile[None, :], axis=1))
    return tile_expert, n_valid.reshape(1).astype(I32), clear.astype(I32)


def kernel(x_prompt, x_sample, state_conv, state_C, state_n, state_m, g_mix, w_in, conv_w, conv_b, b_igate, b_fgate,
           g_mhnorm, w_a, w_spatial, b_spatial, g_vnorm, w_b, w_out, g_ffn, w_router, b_router, w_gate, b_gate,
           w_up, b_up, w_down, b_down, g_final):
    depth = g_mix.shape[0]
    assert depth == 1, "single-layer trunk"
    batch, seq, d = x_prompt.shape
    ns = x_sample.shape[0]
    assert seq % PROMPT_BLOCK == 0 and ns <= PROMPT_BLOCK and TOKEN_BLOCK == PROMPT_BLOCK
    wts = _prepare_weights(g_mix[0], w_in[0], conv_w[0], conv_b[0], b_igate[0], b_fgate[0], g_mhnorm[0], w_a[0],
                           w_spatial[0], b_spatial[0], g_vnorm[0], w_b[0], w_out[0], g_ffn[0], w_router[0],
                           b_router[0])

    x1_s, h2_s, s_conv, s_c, s_n, s_m, vg = _sample_mixer(
        x_sample, state_conv[0], state_C[0], state_n[0], state_m[0], wts)
    x1_all, h2_all, p_conv, p_c, p_n, p_m = _prompt_mixer(x_prompt, x1_s, h2_s, wts)
    n_total = x1_all.shape[0]

    pos, wt, counts = _router(h2_all, wts)
    n_tiles = -(-(n_total * TOP_K) // EXPERT_TILE) + N_EXPERTS
    tile_expert, n_valid, clear = _tile_metadata(counts[:, 0].astype(I32), n_tiles)
    xs = _dispatch(h2_all, pos, clear, n_tiles * EXPERT_TILE)
    ys = _experts(xs, tile_expert, n_valid, w_gate[0], b_gate[0], w_up[0], b_up[0], w_down[0], b_down[0])
    y_prompt, y_sample = _combine(x1_all, pos, wt.T, ys, g_final.reshape(1, d), ns)

    return (y_prompt.reshape(batch, seq, d), y_sample.reshape(ns, 1, d),
            p_conv[None], p_c[None], p_n[None], p_m[None, :, :, 0],
            s_conv[None], s_c[None], s_n[None], s_m[None], vg.reshape(1, ns, 1, N_GROUPS, GROUP_DIM))
```

```python
import functools

import jax
import jax.numpy as jnp
from jax import lax
from jax.experimental import pallas as pl
from jax.experimental.pallas import tpu as pltpu

F32 = jnp.float32
BF16 = jnp.bfloat16
I32 = jnp.int32
HIGHEST = lax.Precision.HIGHEST

N_HEADS = 8
D_K = 64
D_V = 128
D_QK = N_HEADS * D_K
D_VAL = N_HEADS * D_V
CONV_TAPS = 4
GATE_COLS = 2 * N_HEADS
N_GROUPS = 8
GROUP_DIM = 128
SPATIAL_CHUNK = 128
N_EXPERTS = 32
TOP_K = 4
SWIGLU_LIMIT = 7.0
SWIGLU_ALPHA = 1.702
RMS_EPS = 1e-6
EMPTY_MAX = -1e30
LANES = 128
SUBLANES = 8
D_MODEL = 1024
ROW_TILES = D_MODEL // LANES

PROMPT_BLOCK = 256
MLSTM_CHUNK = 256
SAMPLE_STATE_BLOCK = 8
TOKEN_BLOCK = 256
ROW_DMA_UNROLL = 2
EXPERT_TILE = 512
VMEM_LIMIT = 56 * 1024 * 1024


def _dot(a, b, precision=None):
    return jnp.dot(a, b, preferred_element_type=F32, precision=precision)


def _dot_nt(a, b, precision=None):
    return lax.dot_general(a, b, (((1,), (1,)), ((), ())), preferred_element_type=F32, precision=precision)


def _dot_tn(a, b, precision=None):
    return lax.dot_general(a, b, (((0,), (0,)), ((), ())), preferred_element_type=F32, precision=precision)


def _rms(x, g):
    return x * lax.rsqrt(jnp.mean(x * x, axis=-1, keepdims=True) + RMS_EPS) * g


def _sigmoid(x):
    return jax.nn.sigmoid(x)


def _gelu(x):
    return 0.5 * x * (1.0 + lax.erf(x * (2.0 ** -0.5)))


def _log_sigmoid(x):
    return jax.nn.log_sigmoid(x)


def _store_rows(ref, val):
    rows = val.shape[0]
    for jt in range(ROW_TILES):
        ref[pl.ds(jt, rows, stride=ROW_TILES), :] = val[:, jt * LANES:(jt + 1) * LANES]


def _load_rows(ref, rows):
    return jnp.concatenate([ref[pl.ds(jt, rows, stride=ROW_TILES), :] for jt in range(ROW_TILES)], axis=1)


def _row_tile(ref, r):
    return ref.at[pl.ds(pl.multiple_of(r * ROW_TILES, ROW_TILES), ROW_TILES)]


def _in_projection(hn, w_main_ref, piece):
    return _dot(hn, w_main_ref[:, piece * 1024:(piece + 1) * 1024])


def _post_mixer(x, hm, z_o, z_u, z_vg, z_ga, z_gb, spatial_fn,
                g_mh_ref, w_a_ref, g_vn_ref, w_b_ref, w_out_ref, g_ffn_ref, w_rt_ref, b_r_ref):
    parts = []
    for h in range(N_HEADS):
        hh = hm[:, h * D_V:(h + 1) * D_V]
        parts.append(hh * lax.rsqrt(jnp.mean(hh * hh, axis=-1, keepdims=True) + RMS_EPS))
    hmn = jnp.concatenate(parts, axis=1) * g_mh_ref[...]
    y_a = _dot((hmn * _sigmoid(z_o)).astype(BF16), w_a_ref[...])
    u = _gelu(z_u)
    vg = _rms(_gelu(z_vg), g_vn_ref[...])
    s = spatial_fn(vg)
    y_b = _dot((u * s).astype(BF16), w_b_ref[...])
    merged = _sigmoid(z_ga) * y_a + _sigmoid(z_gb) * y_b
    x1 = x + _dot(merged.astype(BF16), w_out_ref[...])
    h2 = _rms(x1, g_ffn_ref[...])
    logits = _dot_nt(w_rt_ref[...], h2, HIGHEST) + b_r_ref[:, 0:1]
    return x1, h2, vg, logits


def _prompt_mixer_kernel(x_ref, x1s_ref, h2s_ref, logits_s_ref, *refs, blocks_per_seq, n_prompt_blocks):
    g = pl.program_id(0)
    b_r_ref, x1_ref, h2_ref, logits_ref = refs[19], refs[20], refs[21], refs[22]

    @pl.when(g < n_prompt_blocks)
    def _():
        _prompt_block(g % blocks_per_seq, x_ref, *refs)

    @pl.when(g == n_prompt_blocks)
    def _():
        ns = x1s_ref.shape[0]
        n_pad = x1_ref.shape[0] - ns
        x1_ref[0:ns, :] = x1s_ref[...]
        x1_ref[ns:, :] = jnp.zeros((n_pad, x1_ref.shape[1]), F32)
        h2_ref[0:ns * ROW_TILES, :] = h2s_ref[...]
        h2_ref[ns * ROW_TILES:, :] = jnp.zeros((n_pad * ROW_TILES, LANES), F32)
        logits_ref[:, 0:ns] = logits_s_ref[...]
        logits_ref[:, ns:] = jnp.broadcast_to(b_r_ref[:, 0:1], (N_EXPERTS, n_pad))


def _prompt_block(j, x_ref, g_mix_ref, w_main_ref, w_ig_ref, w_fg_ref, w_ift_ref, b_ig_ref, b_fg_ref,
                  gate_b_col_ref, conv_w_ref, conv_b_ref, g_mh_ref, w_a_ref, ws_ref, bsp_ref, g_vn_ref,
                  w_b_ref, w_out_ref, g_ffn_ref, w_rt_ref, b_r_ref,
                  x1_ref, h2_ref, logits_ref, pconv_ref, pc_ref, pn_ref, pm_ref,
                  xp_ref):
    tb = x_ref.shape[1]

    @pl.when(j == 0)
    def _():
        pc_ref[...] = jnp.zeros_like(pc_ref)
        pn_ref[...] = jnp.zeros_like(pn_ref)
        pm_ref[...] = jnp.full(pm_ref.shape, EMPTY_MAX, F32)
        xp_ref[0:SUBLANES, :] = jnp.zeros((SUBLANES, xp_ref.shape[1]), F32)

    x = x_ref[0]
    hn = _rms(x, g_mix_ref[...]).astype(BF16)

    z_qk = _in_projection(hn, w_main_ref, 0)
    xp_ref[SUBLANES:SUBLANES + tb, :] = z_qk
    y = conv_b_ref[...] + conv_w_ref[CONV_TAPS - 1:CONV_TAPS, :] * z_qk
    for tap in range(CONV_TAPS - 1):
        back = CONV_TAPS - 1 - tap
        y = y + conv_w_ref[tap:tap + 1, :] * xp_ref[SUBLANES - back:SUBLANES - back + tb, :]
    tail = z_qk[tb - (CONV_TAPS - 1):tb, :]
    xp_ref[SUBLANES - (CONV_TAPS - 1):SUBLANES, :] = tail
    pconv_ref[0] = tail
    qk = y * _sigmoid(y)
    q = qk[:, :D_QK]
    k = qk[:, D_QK:] * (D_K ** -0.5)
    qb = q.astype(BF16)
    v = _in_projection(hn, w_main_ref, 1)
    vb = v.astype(BF16)

    ig_c = _dot(hn, w_ig_ref[...]) + b_ig_ref[...]
    lf_c = _log_sigmoid(_dot(hn, w_fg_ref[...]) + b_fg_ref[...])
    gr = _dot_nt(w_ift_ref[...], hn) + gate_b_col_ref[:, 0:1]
    ig_r = gr[0:N_HEADS, :]
    lf_r = _log_sigmoid(gr[N_HEADS:GATE_COLS, :])
    ch = MLSTM_CHUNK
    row_i = lax.broadcasted_iota(I32, (ch, ch), 0)
    col_i = lax.broadcasted_iota(I32, (ch, ch), 1)
    causal = col_i <= row_i
    tri = causal.astype(F32)

    hm_chunks = []
    for c0 in range(0, tb, ch):
        rows = slice(c0, c0 + ch)
        b_c = _dot(tri, lf_c[rows, :], HIGHEST)
        b_r = _dot_nt(lf_r[:, rows], tri, HIGHEST)
        hm_parts = []
        for h in range(N_HEADS):
            bt = b_c[:, h:h + 1]
            ig_col = ig_c[rows, h:h + 1]
            bs = b_r[h:h + 1, :]
            ig_row = ig_r[h:h + 1, rows]
            m_prev = pm_ref[0, h:h + 1, 0:1]
            c_prev = pc_ref[0, h]
            n_prev = pn_ref[0, h:h + 1, :]
            q_h = q[rows, h * D_K:(h + 1) * D_K]
            qb_h = qb[rows, h * D_K:(h + 1) * D_K]
            k_h = k[rows, h * D_K:(h + 1) * D_K]
            vb_h = vb[rows, h * D_V:(h + 1) * D_V]

            dmat = jnp.where(causal, bt - bs + ig_row, -jnp.inf)
            inter = bt + m_prev
            m_t = jnp.maximum(inter, jnp.max(dmat, axis=-1, keepdims=True))
            w_inter = jnp.exp(inter - m_t)
            s = _dot_nt(qb_h, k_h.astype(BF16)) * jnp.exp(dmat - m_t)
            num = w_inter * _dot(qb_h, c_prev.astype(BF16)) + _dot(s.astype(BF16), vb_h)
            den = w_inter * jnp.sum(q_h * n_prev, axis=-1, keepdims=True) + jnp.sum(s, axis=-1, keepdims=True)
            hm_parts.append(num / jnp.maximum(jnp.abs(den), jnp.exp(-m_t)))

            m_new = m_t[ch - 1:ch, :]
            b_last = bt[ch - 1:ch, :]
            decay = jnp.exp(b_last + m_prev - m_new)
            w_s = jnp.exp(b_last - bt + ig_col - m_new)
            kw = k_h * w_s
            pc_ref[0, h] = decay * c_prev + _dot_tn(kw.astype(BF16), vb_h)
            pn_ref[0, h:h + 1, :] = decay * n_prev + jnp.sum(kw, axis=0, keepdims=True)
            pm_ref[0, h:h + 1, :] = jnp.broadcast_to(m_new, (1, pm_ref.shape[2]))
        hm_chunks.append(jnp.concatenate(hm_parts, axis=1))
    hm = jnp.concatenate(hm_chunks, axis=0)

    def spatial(vg):
        vgb = vg.astype(BF16)
        rows = []
        for c in range(tb // SPATIAL_CHUNK):
            blocks = []
            for g in range(N_GROUPS):
                blk = vgb[c * SPATIAL_CHUNK:(c + 1) * SPATIAL_CHUNK, g * GROUP_DIM:(g + 1) * GROUP_DIM]
                blocks.append(_dot(ws_ref[g], blk))
            rows.append(jnp.concatenate(blocks, axis=1) + bsp_ref[...])
        return jnp.concatenate(rows, axis=0)

    z_o = _in_projection(hn, w_main_ref, 2)
    z_u = _in_projection(hn, w_main_ref, 3)
    z_vg = _in_projection(hn, w_main_ref, 4)
    z_ga = _in_projection(hn, w_main_ref, 5)
    z_gb = _in_projection(hn, w_main_ref, 6)
    x1, h2, _, logits = _post_mixer(x, hm, z_o, z_u, z_vg, z_ga, z_gb, spatial,
                                    g_mh_ref, w_a_ref, g_vn_ref, w_b_ref, w_out_ref, g_ffn_ref, w_rt_ref, b_r_ref)
    x1_ref[...] = x1
    _store_rows(h2_ref, h2)
    logits_ref[...] = logits


def _whole_vmem():
    return pl.BlockSpec(memory_space=pltpu.VMEM)


def _prompt_mixer(x_prompt, x1_sample, h2_sample, logits_sample, wts):
    batch, seq, d = x_prompt.shape
    tb = PROMPT_BLOCK
    nt = seq // tb
    n_blocks = batch * nt
    n_total = (n_blocks + 1) * tb
    out_shapes = (
        jax.ShapeDtypeStruct((n_total, d), F32),
        jax.ShapeDtypeStruct((n_total * ROW_TILES, LANES), F32),
        jax.ShapeDtypeStruct((N_EXPERTS, n_total), F32),
        jax.ShapeDtypeStruct((batch, CONV_TAPS - 1, 2 * D_QK), F32),
        jax.ShapeDtypeStruct((batch, N_HEADS, D_K, D_V), F32),
        jax.ShapeDtypeStruct((batch, N_HEADS, D_K), F32),
        jax.ShapeDtypeStruct((batch, N_HEADS, LANES), F32),
    )
    seq_of = lambda g: jnp.minimum(g, n_blocks - 1) // nt
    row_map = lambda g: (g, 0)
    out_specs = (
        pl.BlockSpec((tb, d), row_map),
        pl.BlockSpec((tb * ROW_TILES, LANES), row_map),
        pl.BlockSpec((N_EXPERTS, tb), lambda g: (0, g)),
        pl.BlockSpec((1, CONV_TAPS - 1, 2 * D_QK), lambda g: (seq_of(g), 0, 0)),
        pl.BlockSpec((1, N_HEADS, D_K, D_V), lambda g: (seq_of(g), 0, 0, 0)),
        pl.BlockSpec((1, N_HEADS, D_K), lambda g: (seq_of(g), 0, 0)),
        pl.BlockSpec((1, N_HEADS, LANES), lambda g: (seq_of(g), 0, 0)),
    )
    in_specs = ([pl.BlockSpec((1, tb, d), lambda g: (seq_of(g), jnp.minimum(g, n_blocks - 1) % nt, 0))]
                + [_whole_vmem()] * 23)
    return pl.pallas_call(
        functools.partial(_prompt_mixer_kernel, blocks_per_seq=nt, n_prompt_blocks=n_blocks),
        grid=(n_blocks + 1,),
        in_specs=in_specs,
        out_specs=out_specs,
        out_shape=out_shapes,
        scratch_shapes=[pltpu.VMEM((SUBLANES + tb, 2 * D_QK), F32)],
        compiler_params=pltpu.CompilerParams(dimension_semantics=("arbitrary",), vmem_limit_bytes=VMEM_LIMIT),
        name="prompt_mixer",
    )(x_prompt, x1_sample, h2_sample, logits_sample,
      wts["g_mix"], wts["w_main"], wts["w_ig"], wts["w_fg"], wts["w_ift"], wts["b_ig"], wts["b_fg"],
      wts["gate_b_col"], wts["conv_w"], wts["conv_b"], wts["g_mh"], wts["w_a"], wts["ws_masked"], wts["bsp_full"],
      wts["g_vn"], wts["w_b"], wts["w_out"], wts["g_ffn"], wts["w_rt"], wts["b_r_col"])


def _sample_pre_kernel(x_ref, c0_ref, c1_ref, c2_ref, n_ref, m_ref,
                       g_mix_ref, w_main_ref, w_ig_ref, w_fg_ref, b_ig_ref, b_fg_ref, conv_w_ref, conv_b_ref,
                       seg_ref, segt_ref,
                       qw_ref, kw_ref, v_ref, scal_ref, nnew_ref, zqk_ref, rest_ref):
    x = x_ref[...]
    hn = _rms(x, g_mix_ref[...]).astype(BF16)
    z_qk = _in_projection(hn, w_main_ref, 0)
    zqk_ref[...] = z_qk
    y = (conv_b_ref[...] + conv_w_ref[0:1, :] * c0_ref[...] + conv_w_ref[1:2, :] * c1_ref[...]
         + conv_w_ref[2:3, :] * c2_ref[...] + conv_w_ref[3:4, :] * z_qk)
    qk = y * _sigmoid(y)
    q = qk[:, :D_QK]
    k = qk[:, D_QK:] * (D_K ** -0.5)
    v_ref[...] = _in_projection(hn, w_main_ref, 1)
    for p in range(5):
        rest_ref[:, p * 1024:(p + 1) * 1024] = _in_projection(hn, w_main_ref, 2 + p)

    ig = _dot(hn, w_ig_ref[...]) + b_ig_ref[...]
    lf = _log_sigmoid(_dot(hn, w_fg_ref[...]) + b_fg_ref[...])
    m_prev = m_ref[...]
    m_new = jnp.maximum(lf + m_prev, ig)
    decay = jnp.exp(lf + m_prev - m_new)
    w_s = jnp.exp(ig - m_new)
    n_prev = n_ref[...]
    seg = seg_ref[...]
    segt = segt_ref[...]
    qk_dot = _dot(q * k, seg, HIGHEST)
    qn_dot = _dot(q * n_prev, seg, HIGHEST)
    s = qk_dot * w_s
    den = decay * qn_dot + s
    denom = jnp.maximum(jnp.abs(den), jnp.exp(-m_new))
    decay_x = _dot(decay, segt, HIGHEST)
    ws_x = _dot(w_s, segt, HIGHEST)
    qw_ref[...] = q * decay_x
    kw = k * ws_x
    kw_ref[...] = kw
    nnew_ref[...] = decay_x * n_prev + kw
    scal_ref[0] = decay
    scal_ref[1] = s
    scal_ref[2] = denom
    scal_ref[3] = m_new


def _sample_state_kernel(c_ref, qw_ref, kw_ref, v_ref, dec_ref, s_ref, den_ref, cnew_ref, h_ref):
    c = c_ref[...]
    v = v_ref[...]
    cnew_ref[...] = dec_ref[...] * c + kw_ref[...] * v
    num = jnp.sum(qw_ref[...] * c, axis=2, keepdims=True) + s_ref[...] * v
    h_ref[...] = num / den_ref[...]


def _sample_post_kernel(x_ref, hm_ref, rest_ref,
                        wsp_ref, bsp_ref, g_mh_ref, w_a_ref, g_vn_ref, w_b_ref, w_out_ref, g_ffn_ref, w_rt_ref,
                        b_r_ref, x1_ref, h2_ref, vg_ref, logits_ref):
    z = [rest_ref[:, p * 1024:(p + 1) * 1024] for p in range(5)]
    spatial = lambda vg: vg * wsp_ref[...] + bsp_ref[...]
    x1, h2, vg, logits = _post_mixer(x_ref[...], hm_ref[...], z[0], z[1], z[2], z[3], z[4], spatial,
                                     g_mh_ref, w_a_ref, g_vn_ref, w_b_ref, w_out_ref, g_ffn_ref, w_rt_ref, b_r_ref)
    x1_ref[...] = x1
    _store_rows(h2_ref, h2)
    vg_ref[...] = vg
    logits_ref[...] = logits


def _sample_mixer(x_sample, state_conv, state_c, state_n, state_m, wts):
    ns, _, d = x_sample.shape
    xs = x_sample.reshape(ns, d)
    c_taps = [state_conv[:, t, :] for t in range(CONV_TAPS - 1)]
    n_rows = state_n.reshape(ns, D_QK)
    pre_out = (
        jax.ShapeDtypeStruct((ns, D_QK), F32),
        jax.ShapeDtypeStruct((ns, D_QK), F32),
        jax.ShapeDtypeStruct((ns, D_VAL), F32),
        jax.ShapeDtypeStruct((4, ns, LANES), F32),
        jax.ShapeDtypeStruct((ns, D_QK), F32),
        jax.ShapeDtypeStruct((ns, 2 * D_QK), F32),
        jax.ShapeDtypeStruct((ns, 5 * 1024), F32),
    )
    qw, kw, v, scal, n_new, z_qk, rest = pl.pallas_call(
        _sample_pre_kernel,
        out_shape=pre_out,
        in_specs=[_whole_vmem()] * 16,
        out_specs=tuple(_whole_vmem() for _ in pre_out),
        compiler_params=pltpu.CompilerParams(vmem_limit_bytes=VMEM_LIMIT),
        name="sample_pre",
    )(xs, c_taps[0], c_taps[1], c_taps[2], n_rows, jnp.pad(state_m, ((0, 0), (0, LANES - N_HEADS))),
      wts["g_mix"], wts["w_main"], wts["w_ig"], wts["w_fg"], wts["b_ig"], wts["b_fg"], wts["conv_w"],
      wts["conv_b"], wts["seg"], wts["segt"])
    scal = scal[:, :, :N_HEADS]

    bb = SAMPLE_STATE_BLOCK
    col4 = lambda a: a.reshape(ns, N_HEADS, D_K, 1)
    sc4 = lambda a: a.reshape(ns, N_HEADS, 1, 1)
    blk = lambda *tail: pl.BlockSpec((bb,) + tail, lambda i: (i, 0, 0, 0))
    c_new, hm = pl.pallas_call(
        _sample_state_kernel,
        grid=(ns // bb,),
        in_specs=[blk(N_HEADS, D_K, D_V), blk(N_HEADS, D_K, 1), blk(N_HEADS, D_K, 1), blk(N_HEADS, 1, D_V),
                  blk(N_HEADS, 1, 1), blk(N_HEADS, 1, 1), blk(N_HEADS, 1, 1)],
        out_specs=(blk(N_HEADS, D_K, D_V), blk(N_HEADS, 1, D_V)),
        out_shape=(jax.ShapeDtypeStruct((ns, N_HEADS, D_K, D_V), F32),
                   jax.ShapeDtypeStruct((ns, N_HEADS, 1, D_V), F32)),
        compiler_params=pltpu.CompilerParams(dimension_semantics=("arbitrary",), vmem_limit_bytes=VMEM_LIMIT),
        name="sample_state",
    )(state_c, col4(qw), col4(kw), v.reshape(ns, N_HEADS, 1, D_V), sc4(scal[0]), sc4(scal[1]), sc4(scal[2]))

    post_out = (jax.ShapeDtypeStruct((ns, d), F32), jax.ShapeDtypeStruct((ns * ROW_TILES, LANES), F32),
                jax.ShapeDtypeStruct((ns, d), F32), jax.ShapeDtypeStruct((N_EXPERTS, ns), F32))
    x1_s, h2_s, vg, logits_s = pl.pallas_call(
        _sample_post_kernel,
        out_shape=post_out,
        in_specs=[_whole_vmem()] * 13,
        out_specs=tuple(_whole_vmem() for _ in post_out),
        compiler_params=pltpu.CompilerParams(vmem_limit_bytes=VMEM_LIMIT),
        name="sample_post",
    )(xs, hm.reshape(ns, D_VAL), rest, wts["wsp_row"], wts["bsp_row"], wts["g_mh"], wts["w_a"], wts["g_vn"],
      wts["w_b"], wts["w_out"], wts["g_ffn"], wts["w_rt"], wts["b_r_col"])
    s_conv = jnp.concatenate([state_conv[:, 1:, :], z_qk[:, None, :]], axis=1)
    return x1_s, h2_s, logits_s, s_conv, c_new, n_new.reshape(ns, N_HEADS, D_K), scal[3], vg


def _router_kernel(logits_ref, pos_ref, wt_ref, cnt_ref, run_ref, off_ref):
    phase = pl.program_id(0)
    i = pl.program_id(1)
    tb = logits_ref.shape[1]

    @pl.when(jnp.logical_and(phase == 0, i == 0))
    def _():
        cnt_ref[...] = jnp.zeros_like(cnt_ref)

    logits = logits_ref[...]
    e_iota = lax.broadcasted_iota(I32, logits.shape, 0)
    sels, vals = [], []
    work = logits
    for _ in range(TOP_K):
        mx = jnp.max(work, axis=0, keepdims=True)
        idx = jnp.min(jnp.where(work == mx, e_iota, N_EXPERTS), axis=0, keepdims=True)
        sel = e_iota == idx
        work = jnp.where(sel, -jnp.inf, work)
        sels.append(sel.astype(F32))
        vals.append(mx)
    sel_all = sels[0] + sels[1] + sels[2] + sels[3]
    block_count = jnp.sum(sel_all, axis=1, keepdims=True)

    @pl.when(phase == 0)
    def _():
        cnt_ref[...] += jnp.broadcast_to(block_count, cnt_ref.shape)

    @pl.when(jnp.logical_and(phase == 1, i == 0))
    def _():
        cnt = cnt_ref[...]
        padded = jnp.ceil(cnt * (1.0 / EXPERT_TILE)) * EXPERT_TILE
        r = lax.broadcasted_iota(I32, (N_EXPERTS, N_EXPERTS), 0)
        c = lax.broadcasted_iota(I32, (N_EXPERTS, N_EXPERTS), 1)
        off_ref[...] = _dot((c < r).astype(F32), padded, HIGHEST)
        run_ref[...] = jnp.zeros_like(run_ref)

    @pl.when(phase == 1)
    def _():
        r = lax.broadcasted_iota(I32, (tb, tb), 0)
        c = lax.broadcasted_iota(I32, (tb, tb), 1)
        before = _dot(sel_all.astype(BF16), (r < c).astype(BF16))
        row = before + run_ref[:, 0:1] + off_ref[:, 0:1]
        es = [jnp.exp(vv - vals[0]) for vv in vals]
        tot = es[0] + es[1] + es[2] + es[3]
        for kk in range(TOP_K):
            pos_ref[kk:kk + 1, :] = jnp.sum(sels[kk] * row, axis=0, keepdims=True).astype(I32)
            wt_ref[kk:kk + 1, :] = es[kk] / tot
        run_ref[...] += jnp.broadcast_to(block_count, run_ref.shape)


def _router(logits):
    n = logits.shape[1]
    tb = PROMPT_BLOCK
    nb = n // tb
    return pl.pallas_call(
        _router_kernel,
        grid=(2, nb),
        in_specs=[pl.BlockSpec((N_EXPERTS, tb), lambda p, i: (0, i))],
        out_specs=(pl.BlockSpec((TOP_K, tb), lambda p, i: (0, i * p)),
                   pl.BlockSpec((TOP_K, tb), lambda p, i: (0, i * p)),
                   pl.BlockSpec((N_EXPERTS, LANES), lambda p, i: (0, 0))),
        out_shape=(jax.ShapeDtypeStruct((TOP_K, n), I32), jax.ShapeDtypeStruct((TOP_K, n), F32),
                   jax.ShapeDtypeStruct((N_EXPERTS, LANES), F32)),
        scratch_shapes=[pltpu.VMEM((N_EXPERTS, LANES), F32), pltpu.VMEM((N_EXPERTS, LANES), F32)],
        compiler_params=pltpu.CompilerParams(
            dimension_semantics=("arbitrary", "arbitrary"), vmem_limit_bytes=VMEM_LIMIT),
        name="router",
    )(logits)


def _row_dma_loops(n_tokens, make_copy):
    def each(t0, fn):
        for u in range(ROW_DMA_UNROLL):
            for kk in range(TOP_K):
                fn(make_copy(t0 * ROW_DMA_UNROLL + u, kk), kk)

    def start(t0, carry):
        each(t0, lambda c, kk: c.start(priority=kk % 2))
        return carry

    def wait(t0, carry):
        each(t0, lambda c, kk: c.wait())
        return carry

    lax.fori_loop(0, n_tokens // ROW_DMA_UNROLL, start, 0)
    lax.fori_loop(0, n_tokens // ROW_DMA_UNROLL, wait, 0)


def _dispatch_kernel(clear_ref, pos_ref, h2_ref, xs_ref, zero_ref, sem, zsem):
    i = pl.program_id(0)
    tb = pos_ref.shape[1]
    tile_rows = EXPERT_TILE * ROW_TILES
    n_tiles = xs_ref.shape[0] // tile_rows

    @pl.when(i == 0)
    def _():
        zero_ref[...] = jnp.zeros_like(zero_ref)

        def tile_copy(t):
            start_row = pl.multiple_of(t * tile_rows, tile_rows)
            return pltpu.make_async_copy(zero_ref, xs_ref.at[pl.ds(start_row, tile_rows)], zsem)

        def start(t, carry):
            @pl.when(clear_ref[t] > 0)
            def _():
                tile_copy(t).start()
            return carry

        def wait(t, carry):
            @pl.when(clear_ref[t] > 0)
            def _():
                tile_copy(t).wait()
            return carry

        lax.fori_loop(0, n_tiles, start, 0)
        lax.fori_loop(0, n_tiles, wait, 0)

    _row_dma_loops(tb, lambda t, kk: pltpu.make_async_copy(
        _row_tile(h2_ref, t), _row_tile(xs_ref, pos_ref[kk, t]), sem))


def _dispatch(h2_all, pos, clear, n_rows):
    n = h2_all.shape[0] // ROW_TILES
    tb = TOKEN_BLOCK
    return pl.pallas_call(
        _dispatch_kernel,
        grid_spec=pltpu.PrefetchScalarGridSpec(
            num_scalar_prefetch=1,
            grid=(n // tb,),
            in_specs=[pl.BlockSpec((TOP_K, tb), lambda i, clr: (0, i), memory_space=pltpu.SMEM),
                      pl.BlockSpec((tb * ROW_TILES, LANES), lambda i, clr: (i, 0))],
            out_specs=pl.BlockSpec(memory_space=pl.ANY),
            scratch_shapes=[pltpu.VMEM((EXPERT_TILE * ROW_TILES, LANES), F32), pltpu.SemaphoreType.DMA,
                            pltpu.SemaphoreType.DMA],
        ),
        out_shape=jax.ShapeDtypeStruct((n_rows * ROW_TILES, LANES), F32),
        compiler_params=pltpu.CompilerParams(dimension_semantics=("arbitrary",), vmem_limit_bytes=VMEM_LIMIT),
        name="dispatch",
    )(clear, pos, h2_all)


def _expert_kernel(tile_e_ref, nvalid_ref, xs_ref, wg_ref, bg_ref, wu_ref, bu_ref, wd_ref, bd_ref, ys_ref,
                   wg_b, wu_b, wd_b):
    i = pl.program_id(0)
    prev = tile_e_ref[jnp.maximum(i - 1, 0)]
    fresh = jnp.logical_or(i == 0, tile_e_ref[i] != prev)

    @pl.when(jnp.logical_and(i < nvalid_ref[0], fresh))
    def _():
        wg_b[...] = wg_ref[0].astype(BF16)
        wu_b[...] = wu_ref[0].astype(BF16)
        wd_b[...] = wd_ref[0].astype(BF16)

    @pl.when(i < nvalid_ref[0])
    def _():
        x = _load_rows(xs_ref, EXPERT_TILE).astype(BF16)
        gate = jnp.minimum(_dot(x, wg_b[...]) + bg_ref[0], SWIGLU_LIMIT)
        up = jnp.clip(_dot(x, wu_b[...]) + bu_ref[0], -SWIGLU_LIMIT, SWIGLU_LIMIT)
        glu = gate * _sigmoid(SWIGLU_ALPHA * gate)
        _store_rows(ys_ref, _dot(((up + 1.0) * glu).astype(BF16), wd_b[...]) + bd_ref[0])

    @pl.when(i >= nvalid_ref[0])
    def _():
        ys_ref[...] = jnp.zeros_like(ys_ref)


def _experts(xs, tile_expert, n_valid, w_gate, b_gate, w_up, b_up, w_down, b_down):
    d, d_ff = w_gate.shape[1], w_gate.shape[2]
    tm = EXPERT_TILE
    n_tiles = xs.shape[0] // (tm * ROW_TILES)
    row_map = lambda i, te, nv: (jnp.minimum(i, nv[0] - 1), 0)
    w_map = lambda i, te, nv: (te[i], 0, 0)
    return pl.pallas_call(
        _expert_kernel,
        grid_spec=pltpu.PrefetchScalarGridSpec(
            num_scalar_prefetch=2,
            grid=(n_tiles,),
            in_specs=[pl.BlockSpec((tm * ROW_TILES, LANES), row_map),
                      pl.BlockSpec((1, d, d_ff), w_map), pl.BlockSpec((1, 1, d_ff), w_map),
                      pl.BlockSpec((1, d, d_ff), w_map), pl.BlockSpec((1, 1, d_ff), w_map),
                      pl.BlockSpec((1, d_ff, d), w_map), pl.BlockSpec((1, 1, d), w_map)],
            out_specs=pl.BlockSpec((tm * ROW_TILES, LANES), lambda i, te, nv: (i, 0)),
            scratch_shapes=[pltpu.VMEM((d, d_ff), BF16), pltpu.VMEM((d, d_ff), BF16), pltpu.VMEM((d_ff, d), BF16)],
        ),
        out_shape=jax.ShapeDtypeStruct(xs.shape, F32),
        compiler_params=pltpu.CompilerParams(dimension_semantics=("arbitrary",), vmem_limit_bytes=VMEM_LIMIT),
        name="experts",
    )(tile_expert, n_valid, xs, w_gate, b_gate[:, None, :], w_up, b_up[:, None, :], w_down, b_down[:, None, :])


def _combine_kernel(pos_ref, x1_ref, wt_ref, g_final_ref, ys_ref, yp_ref, ysmp_ref, buf_ref, sem):
    i = pl.program_id(0)
    last = pl.num_programs(0) - 1
    tb = x1_ref.shape[0]
    _row_dma_loops(tb, lambda t, kk: pltpu.make_async_copy(
        _row_tile(ys_ref, pos_ref[kk, t]), _row_tile(buf_ref.at[kk], t), sem))
    out = x1_ref[...]
    for kk in range(TOP_K):
        out = out + wt_ref[:, kk:kk + 1] * _load_rows(buf_ref.at[kk], tb)
    y = _rms(out, g_final_ref[...])

    @pl.when(i < last)
    def _():
        yp_ref[...] = y

    @pl.when(i == last)
    def _():
        ysmp_ref[...] = y[0:ysmp_ref.shape[0], :]


def _combine(x1_all, pos, wt_cols, ys, g_final_row, n_prompt, ns):
    d = x1_all.shape[1]
    tb = TOKEN_BLOCK
    nb = n_prompt // tb + 1
    return pl.pallas_call(
        _combine_kernel,
        grid=(nb,),
        in_specs=[pl.BlockSpec((TOP_K, tb), lambda i: (0, i), memory_space=pltpu.SMEM),
                  pl.BlockSpec((tb, d), lambda i: (i, 0)),
                  pl.BlockSpec((tb, TOP_K), lambda i: (i, 0)),
                  pl.BlockSpec((1, d), lambda i: (0, 0)),
                  pl.BlockSpec(memory_space=pl.ANY)],
        out_specs=(pl.BlockSpec((tb, d), lambda i: (jnp.minimum(i, nb - 2), 0)),
                   pl.BlockSpec((ns, d), lambda i: (0, 0))),
        out_shape=(jax.ShapeDtypeStruct((n_prompt, d), F32), jax.ShapeDtypeStruct((ns, d), F32)),
        scratch_shapes=[pltpu.VMEM((TOP_K, tb * ROW_TILES, LANES), F32), pltpu.SemaphoreType.DMA],
        compiler_params=pltpu.CompilerParams(dimension_semantics=("arbitrary",), vmem_limit_bytes=VMEM_LIMIT),
        name="combine",
    )(pos, x1_all, wt_cols, g_final_row, ys)


def _prepare_weights(g_mix, w_in, conv_w, conv_b, b_igate, b_fgate, g_mhnorm, w_a, w_spatial, b_spatial, g_vnorm,
                     w_b, w_out, g_ffn, w_router, b_router):
    d = w_in.shape[0]
    gate_lo = 2 * D_QK + D_VAL
    gate_hi = gate_lo + GATE_COLS
    row = lambda a: a.reshape(1, -1).astype(F32)
    w_if = w_in[:, gate_lo:gate_hi]
    gate_b = jnp.concatenate([b_igate, b_fgate])
    tril = jnp.tril(jnp.ones((SPATIAL_CHUNK, SPATIAL_CHUNK), w_spatial.dtype))
    head_of_lane = jnp.arange(D_QK) // D_K
    seg = (head_of_lane[:, None] == jnp.arange(LANES)[None, :]).astype(F32)
    pad_cols = lambda a: jnp.pad(a, ((0, 0), (0, LANES - N_HEADS)))
    return dict(
        g_mix=row(g_mix),
        w_main=jnp.concatenate([w_in[:, :gate_lo], w_in[:, gate_hi:]], axis=1).astype(BF16),
        w_ig=pad_cols(w_if[:, :N_HEADS]).astype(BF16),
        w_fg=pad_cols(w_if[:, N_HEADS:]).astype(BF16),
        w_ift=w_if.T.astype(BF16),
        b_ig=pad_cols(b_igate.reshape(1, N_HEADS)),
        b_fg=pad_cols(b_fgate.reshape(1, N_HEADS)),
        gate_b_col=jnp.broadcast_to(gate_b[:, None], (GATE_COLS, LANES)),
        conv_w=conv_w, conv_b=row(conv_b), g_mh=row(g_mhnorm), w_a=w_a.astype(BF16),
        ws_masked=(w_spatial * tril).astype(BF16),
        bsp_full=jnp.repeat(b_spatial.T, GROUP_DIM, axis=1),
        wsp_row=jnp.repeat(w_spatial[:, 0, 0], GROUP_DIM).reshape(1, -1),
        bsp_row=jnp.repeat(b_spatial[:, 0], GROUP_DIM).reshape(1, -1),
        g_vn=row(g_vnorm), w_b=w_b.astype(BF16), w_out=w_out.astype(BF16), g_ffn=row(g_ffn),
        w_rt=w_router.T, b_r_col=jnp.broadcast_to(b_router[:, None], (N_EXPERTS, LANES)),
        seg=seg, segt=seg.T,
    )


def _tile_metadata(counts, n_tiles):
    tiles_per_expert = (counts + EXPERT_TILE - 1) // EXPERT_TILE
    tile_end = jnp.cumsum(tiles_per_expert)
    n_valid = tile_end[-1]
    tile_ids = jnp.minimum(jnp.arange(n_tiles, dtype=I32), n_valid - 1)
    tile_expert = jnp.sum(tile_end[None, :] <= tile_ids[:, None], axis=1).astype(I32)
    tail_tile = jnp.where(counts > 0, tile_end - 1, -1)
    all_tiles = jnp.arange(n_tiles, dtype=I32)
    clear = jnp.logical_or(all_tiles >= n_valid, jnp.any(all_tiles[:, None] == tail_tile[None, :], axis=1))
    return tile_expert, n_valid.reshape(1).astype(I32), clear.astype(I32)


def kernel(x_prompt, x_sample, state_conv, state_C, state_n, state_m, g_mix, w_in, conv_w, conv_b, b_igate, b_fgate,
           g_mhnorm, w_a, w_spatial, b_spatial, g_vnorm, w_b, w_out, g_ffn, w_router, b_router, w_gate, b_gate,
           w_up, b_up, w_down, b_down, g_final):
    depth = g_mix.shape[0]
    assert depth == 1, "single-layer trunk"
    batch, seq, d = x_prompt.shape
    ns = x_sample.shape[0]
    assert seq % PROMPT_BLOCK == 0 and ns <= TOKEN_BLOCK and PROMPT_BLOCK % TOKEN_BLOCK == 0
    assert PROMPT_BLOCK % MLSTM_CHUNK == 0 and MLSTM_CHUNK % SPATIAL_CHUNK == 0
    wts = _prepare_weights(g_mix[0], w_in[0], conv_w[0], conv_b[0], b_igate[0], b_fgate[0], g_mhnorm[0], w_a[0],
                           w_spatial[0], b_spatial[0], g_vnorm[0], w_b[0], w_out[0], g_ffn[0], w_router[0],
                           b_router[0])

    x1_s, h2_s, logits_s, s_conv, s_c, s_n, s_m, vg = _sample_mixer(
        x_sample, state_conv[0], state_C[0], state_n[0], state_m[0], wts)
    x1_all, h2_all, logits, p_conv, p_c, p_n, p_m = _prompt_mixer(x_prompt, x1_s, h2_s, logits_s, wts)
    n_total = x1_all.shape[0]

    pos, wt, counts = _router(logits)
    n_tiles = -(-(n_total * TOP_K) // EXPERT_TILE) + N_EXPERTS
    tile_expert, n_valid, clear = _tile_metadata(counts[:, 0].astype(I32), n_tiles)
    xs = _dispatch(h2_all, pos, clear, n_tiles * EXPERT_TILE)
    ys = _experts(xs, tile_expert, n_valid, w_gate[0], b_gate[0], w_up[0], b_up[0], w_down[0], b_down[0])
    y_prompt, y_sample = _combine(x1_all, pos, wt.T, ys, g_final.reshape(1, d), batch * seq, ns)

    return (y_prompt.reshape(batch, seq, d), y_sample.reshape(ns, 1, d),
            p_conv[None], p_c[None], p_n[None], p_m[None, :, :, 0],
            s_conv[None], s_c[None], s_n[None], s_m[None], vg.reshape(1, ns, 1, N_GROUPS, GROUP_DIM))
```

```python
import functools

import jax
import jax.numpy as jnp
from jax import lax
from jax.experimental import pallas as pl
from jax.experimental.pallas import tpu as pltpu

F32 = jnp.float32
BF16 = jnp.bfloat16
I32 = jnp.int32
HIGHEST = lax.Precision.HIGHEST

N_HEADS = 8
D_K = 64
D_V = 128
D_QK = N_HEADS * D_K
D_VAL = N_HEADS * D_V
CONV_TAPS = 4
GATE_COLS = 2 * N_HEADS
N_GROUPS = 8
GROUP_DIM = 128
SPATIAL_CHUNK = 128
N_EXPERTS = 32
TOP_K = 4
SWIGLU_LIMIT = 7.0
SWIGLU_ALPHA = 1.702
RMS_EPS = 1e-6
EMPTY_MAX = -1e30
LANES = 128
SUBLANES = 8
D_MODEL = 1024
ROW_TILES = D_MODEL // LANES

PROMPT_BLOCK = 256
MLSTM_CHUNK = 256
SEQS_PER_STEP = 1
SAMPLE_STATE_BLOCK = 8
TOKEN_BLOCK = 256
ROW_DMA_UNROLL = 2
ROUTER_BLOCK_MAX = 1280
EXPERT_TILE = 512
VMEM_LIMIT = 56 * 1024 * 1024


def _dot(a, b, precision=None):
    return jnp.dot(a, b, preferred_element_type=F32, precision=precision)


def _dot_nt(a, b, precision=None):
    return lax.dot_general(a, b, (((1,), (1,)), ((), ())), preferred_element_type=F32, precision=precision)


def _dot_tn(a, b, precision=None):
    return lax.dot_general(a, b, (((0,), (0,)), ((), ())), preferred_element_type=F32, precision=precision)


def _rms(x, g):
    return x * lax.rsqrt(jnp.mean(x * x, axis=-1, keepdims=True) + RMS_EPS) * g


def _sigmoid(x):
    return jax.nn.sigmoid(x)


def _gelu(x):
    return 0.5 * x * (1.0 + lax.erf(x * (2.0 ** -0.5)))


def _log_sigmoid(x):
    return jax.nn.log_sigmoid(x)


def _store_rows(ref, val):
    rows = val.shape[0]
    for jt in range(ROW_TILES):
        ref[pl.ds(jt, rows, stride=ROW_TILES), :] = val[:, jt * LANES:(jt + 1) * LANES]


def _load_rows(ref, rows):
    return jnp.concatenate([ref[pl.ds(jt, rows, stride=ROW_TILES), :] for jt in range(ROW_TILES)], axis=1)


def _row_tile(ref, r):
    return ref.at[pl.ds(pl.multiple_of(r * ROW_TILES, ROW_TILES), ROW_TILES)]


def _in_projection(hn, w_main_ref, piece):
    return _dot(hn, w_main_ref[:, piece * 1024:(piece + 1) * 1024])


def _post_mixer(x, hm, z_o, z_u, z_vg, z_ga, z_gb, spatial_fn,
                g_mh_ref, w_a_ref, g_vn_ref, w_b_ref, w_out_ref):
    parts = []
    for h in range(N_HEADS):
        hh = hm[:, h * D_V:(h + 1) * D_V]
        parts.append(hh * lax.rsqrt(jnp.mean(hh * hh, axis=-1, keepdims=True) + RMS_EPS))
    hmn = jnp.concatenate(parts, axis=1) * g_mh_ref[...]
    y_a = _dot((hmn * _sigmoid(z_o)).astype(BF16), w_a_ref[...])
    u = _gelu(z_u)
    vg = _rms(_gelu(z_vg), g_vn_ref[...])
    s = spatial_fn(vg)
    y_b = _dot((u * s).astype(BF16), w_b_ref[...])
    merged = _sigmoid(z_ga) * y_a + _sigmoid(z_gb) * y_b
    x1 = x + _dot(merged.astype(BF16), w_out_ref[...])
    return x1, vg


def _ffn_input(x1, g_ffn_ref, w_rt_ref, b_r_ref):
    h2 = _rms(x1, g_ffn_ref[...])
    return h2, _dot_nt(w_rt_ref[...], h2, HIGHEST) + b_r_ref[:, 0:1]


def _prompt_mixer_kernel(x_ref, x1s_ref, h2s_ref, logits_s_ref, *refs, blocks_per_seq, n_prompt_steps):
    g = pl.program_id(0)
    block_weights = refs[:17]
    g_ffn_ref, w_rt_ref, b_r_ref = refs[17:20]
    x1_ref, h2_ref, logits_ref, pconv_ref, pc_ref, pn_ref, pm_ref, xp_ref, x1_keep_ref = refs[20:]
    tb = x_ref.shape[1]

    def finish_previous_block():
        for s in range(SEQS_PER_STEP):
            h2, logits = _ffn_input(x1_keep_ref[s], g_ffn_ref, w_rt_ref, b_r_ref)
            _store_rows(h2_ref.at[pl.ds(s * tb * ROW_TILES, tb * ROW_TILES)], h2)
            logits_ref[:, s * tb:(s + 1) * tb] = logits

    @pl.when(g == 0)
    def _():
        x1_keep_ref[...] = jnp.zeros_like(x1_keep_ref)

    @pl.when(jnp.logical_and(g < n_prompt_steps, g % blocks_per_seq == 0))
    def _():
        pc_ref[...] = jnp.zeros_like(pc_ref)
        pn_ref[...] = jnp.zeros_like(pn_ref)
        pm_ref[...] = jnp.full(pm_ref.shape, EMPTY_MAX, F32)
        xp_ref[:, 0:SUBLANES, :] = jnp.zeros((xp_ref.shape[0], SUBLANES, xp_ref.shape[2]), F32)

    @pl.when(g < n_prompt_steps)
    def _():
        finish_previous_block()
        for s in range(SEQS_PER_STEP):
            _prompt_block(x_ref.at[s], *block_weights, x1_ref.at[pl.ds(s * tb, tb)],
                          pconv_ref.at[s], pc_ref.at[s], pn_ref.at[s], pm_ref.at[s], xp_ref.at[s], x1_keep_ref.at[s])

    ns = x1s_ref.shape[0]
    n_pad = x1_ref.shape[0] - ns

    @pl.when(g == n_prompt_steps)
    def _():
        finish_previous_block()
        x1_ref[0:ns, :] = x1s_ref[...]
        x1_ref[ns:, :] = jnp.zeros((n_pad, x1_ref.shape[1]), F32)

    @pl.when(g == n_prompt_steps + 1)
    def _():
        h2_ref[0:ns * ROW_TILES, :] = h2s_ref[...]
        h2_ref[ns * ROW_TILES:, :] = jnp.zeros((n_pad * ROW_TILES, LANES), F32)
        logits_ref[:, 0:ns] = logits_s_ref[...]
        logits_ref[:, ns:] = jnp.broadcast_to(b_r_ref[:, 0:1], (N_EXPERTS, n_pad))


def _prompt_block(x_ref, g_mix_ref, w_main_ref, w_ig_ref, w_fg_ref, w_ift_ref, b_ig_ref, b_fg_ref,
                  gate_b_col_ref, conv_w_ref, conv_b_ref, g_mh_ref, w_a_ref, ws_ref, bsp_ref, g_vn_ref,
                  w_b_ref, w_out_ref,
                  x1_ref, pconv_ref, pc_ref, pn_ref, pm_ref, xp_ref, x1_keep_ref):
    tb = x_ref.shape[0]
    x = x_ref[...]
    hn = _rms(x, g_mix_ref[...]).astype(BF16)

    z_qk = _in_projection(hn, w_main_ref, 0)
    xp_ref[SUBLANES:SUBLANES + tb, :] = z_qk
    y = conv_b_ref[...] + conv_w_ref[CONV_TAPS - 1:CONV_TAPS, :] * z_qk
    for tap in range(CONV_TAPS - 1):
        back = CONV_TAPS - 1 - tap
        y = y + conv_w_ref[tap:tap + 1, :] * xp_ref[SUBLANES - back:SUBLANES - back + tb, :]
    tail = z_qk[tb - (CONV_TAPS - 1):tb, :]
    xp_ref[SUBLANES - (CONV_TAPS - 1):SUBLANES, :] = tail
    pconv_ref[...] = tail
    qk = y * _sigmoid(y)
    q = qk[:, :D_QK]
    k = qk[:, D_QK:] * (D_K ** -0.5)
    qb = q.astype(BF16)
    v = _in_projection(hn, w_main_ref, 1)
    vb = v.astype(BF16)

    ig_c = _dot(hn, w_ig_ref[...]) + b_ig_ref[...]
    lf_c = _log_sigmoid(_dot(hn, w_fg_ref[...]) + b_fg_ref[...])
    gr = _dot_nt(w_ift_ref[...], hn) + gate_b_col_ref[:, 0:1]
    ig_r = gr[0:N_HEADS, :]
    lf_r = _log_sigmoid(gr[N_HEADS:GATE_COLS, :])
    ch = MLSTM_CHUNK
    row_i = lax.broadcasted_iota(I32, (ch, ch), 0)
    col_i = lax.broadcasted_iota(I32, (ch, ch), 1)
    causal = col_i <= row_i
    tri = causal.astype(F32)

    hm_chunks = []
    for c0 in range(0, tb, ch):
        rows = slice(c0, c0 + ch)
        b_c = _dot(tri, lf_c[rows, :], HIGHEST)
        b_r = _dot_nt(lf_r[:, rows], tri, HIGHEST)
        hm_parts = []
        for h in range(N_HEADS):
            bt = b_c[:, h:h + 1]
            ig_col = ig_c[rows, h:h + 1]
            bs = b_r[h:h + 1, :]
            ig_row = ig_r[h:h + 1, rows]
            m_prev = pm_ref[h:h + 1, 0:1]
            c_prev = pc_ref[h]
            n_prev = pn_ref[h:h + 1, :]
            q_h = q[rows, h * D_K:(h + 1) * D_K]
            qb_h = qb[rows, h * D_K:(h + 1) * D_K]
            k_h = k[rows, h * D_K:(h + 1) * D_K]
            vb_h = vb[rows, h * D_V:(h + 1) * D_V]

            dmat = jnp.where(causal, bt - bs + ig_row, -jnp.inf)
            inter = bt + m_prev
            m_t = jnp.maximum(inter, jnp.max(dmat, axis=-1, keepdims=True))
            w_inter = jnp.exp(inter - m_t)
            s = _dot_nt(qb_h, k_h.astype(BF16)) * jnp.exp(dmat - m_t)
            num = w_inter * _dot(qb_h, c_prev.astype(BF16)) + _dot(s.astype(BF16), vb_h)
            den = w_inter * jnp.sum(q_h * n_prev, axis=-1, keepdims=True) + jnp.sum(s, axis=-1, keepdims=True)
            hm_parts.append(num / jnp.maximum(jnp.abs(den), jnp.exp(-m_t)))

            m_new = m_t[ch - 1:ch, :]
            b_last = bt[ch - 1:ch, :]
            decay = jnp.exp(b_last + m_prev - m_new)
            w_s = jnp.exp(b_last - bt + ig_col - m_new)
            kw = k_h * w_s
            pc_ref[h] = decay * c_prev + _dot_tn(kw.astype(BF16), vb_h)
            pn_ref[h:h + 1, :] = decay * n_prev + jnp.sum(kw, axis=0, keepdims=True)
            pm_ref[h:h + 1, :] = jnp.broadcast_to(m_new, (1, pm_ref.shape[1]))
        hm_chunks.append(jnp.concatenate(hm_parts, axis=1))
    hm = jnp.concatenate(hm_chunks, axis=0)

    def spatial(vg):
        vgb = vg.astype(BF16)
        rows = []
        for c in range(tb // SPATIAL_CHUNK):
            blocks = []
            for g in range(N_GROUPS):
                blk = vgb[c * SPATIAL_CHUNK:(c + 1) * SPATIAL_CHUNK, g * GROUP_DIM:(g + 1) * GROUP_DIM]
                blocks.append(_dot(ws_ref[g], blk))
            rows.append(jnp.concatenate(blocks, axis=1) + bsp_ref[...])
        return jnp.concatenate(rows, axis=0)

    z_o = _in_projection(hn, w_main_ref, 2)
    z_u = _in_projection(hn, w_main_ref, 3)
    z_vg = _in_projection(hn, w_main_ref, 4)
    z_ga = _in_projection(hn, w_main_ref, 5)
    z_gb = _in_projection(hn, w_main_ref, 6)
    x1, _ = _post_mixer(x, hm, z_o, z_u, z_vg, z_ga, z_gb, spatial, g_mh_ref, w_a_ref, g_vn_ref, w_b_ref, w_out_ref)
    x1_ref[...] = x1
    x1_keep_ref[...] = x1


def _whole_vmem():
    return pl.BlockSpec(memory_space=pltpu.VMEM)


def _prompt_mixer(x_prompt, x1_sample, h2_sample, logits_sample, wts):
    batch, seq, d = x_prompt.shape
    tb = PROMPT_BLOCK
    sps = SEQS_PER_STEP
    nt = seq // tb
    n_steps = (batch // sps) * nt
    step_rows = sps * tb
    n_total = (n_steps + 1) * step_rows
    out_shapes = (
        jax.ShapeDtypeStruct((n_total, d), F32),
        jax.ShapeDtypeStruct((n_total * ROW_TILES, LANES), F32),
        jax.ShapeDtypeStruct((N_EXPERTS, n_total), F32),
        jax.ShapeDtypeStruct((batch, CONV_TAPS - 1, 2 * D_QK), F32),
        jax.ShapeDtypeStruct((batch, N_HEADS, D_K, D_V), F32),
        jax.ShapeDtypeStruct((batch, N_HEADS, D_K), F32),
        jax.ShapeDtypeStruct((batch, N_HEADS, LANES), F32),
    )
    group_of = lambda g: jnp.minimum(g, n_steps - 1) // nt
    lagged = lambda g: jnp.clip(g - 1, 0, n_steps)
    out_specs = (
        pl.BlockSpec((step_rows, d), lambda g: (jnp.minimum(g, n_steps), 0)),
        pl.BlockSpec((step_rows * ROW_TILES, LANES), lambda g: (lagged(g), 0)),
        pl.BlockSpec((N_EXPERTS, step_rows), lambda g: (0, lagged(g))),
        pl.BlockSpec((sps, CONV_TAPS - 1, 2 * D_QK), lambda g: (group_of(g), 0, 0)),
        pl.BlockSpec((sps, N_HEADS, D_K, D_V), lambda g: (group_of(g), 0, 0, 0)),
        pl.BlockSpec((sps, N_HEADS, D_K), lambda g: (group_of(g), 0, 0)),
        pl.BlockSpec((sps, N_HEADS, LANES), lambda g: (group_of(g), 0, 0)),
    )
    in_specs = ([pl.BlockSpec((sps, tb, d), lambda g: (group_of(g), jnp.minimum(g, n_steps - 1) % nt, 0))]
                + [_whole_vmem()] * 23)
    return pl.pallas_call(
        functools.partial(_prompt_mixer_kernel, blocks_per_seq=nt, n_prompt_steps=n_steps),
        grid=(n_steps + 2,),
        in_specs=in_specs,
        out_specs=out_specs,
        out_shape=out_shapes,
        scratch_shapes=[pltpu.VMEM((sps, SUBLANES + tb, 2 * D_QK), F32), pltpu.VMEM((sps, tb, d), F32)],
        compiler_params=pltpu.CompilerParams(dimension_semantics=("arbitrary",), vmem_limit_bytes=VMEM_LIMIT),
        name="prompt_mixer",
    )(x_prompt, x1_sample, h2_sample, logits_sample,
      wts["g_mix"], wts["w_main"], wts["w_ig"], wts["w_fg"], wts["w_ift"], wts["b_ig"], wts["b_fg"],
      wts["gate_b_col"], wts["conv_w"], wts["conv_b"], wts["g_mh"], wts["w_a"], wts["ws_masked"], wts["bsp_full"],
      wts["g_vn"], wts["w_b"], wts["w_out"], wts["g_ffn"], wts["w_rt"], wts["b_r_col"])


def _sample_pre_kernel(x_ref, c0_ref, c1_ref, c2_ref, n_ref, m_ref,
                       g_mix_ref, w_main_ref, w_ig_ref, w_fg_ref, b_ig_ref, b_fg_ref, conv_w_ref, conv_b_ref,
                       seg_ref, segt_ref,
                       qw_ref, kw_ref, v_ref, scal_ref, nnew_ref, zqk_ref, rest_ref):
    x = x_ref[...]
    hn = _rms(x, g_mix_ref[...]).astype(BF16)
    z_qk = _in_projection(hn, w_main_ref, 0)
    zqk_ref[...] = z_qk
    y = (conv_b_ref[...] + conv_w_ref[0:1, :] * c0_ref[...] + conv_w_ref[1:2, :] * c1_ref[...]
         + conv_w_ref[2:3, :] * c2_ref[...] + conv_w_ref[3:4, :] * z_qk)
    qk = y * _sigmoid(y)
    q = qk[:, :D_QK]
    k = qk[:, D_QK:] * (D_K ** -0.5)
    v_ref[...] = _in_projection(hn, w_main_ref, 1)
    for p in range(5):
        rest_ref[:, p * 1024:(p + 1) * 1024] = _in_projection(hn, w_main_ref, 2 + p)

    ig = _dot(hn, w_ig_ref[...]) + b_ig_ref[...]
    lf = _log_sigmoid(_dot(hn, w_fg_ref[...]) + b_fg_ref[...])
    m_prev = m_ref[...]
    m_new = jnp.maximum(lf + m_prev, ig)
    decay = jnp.exp(lf + m_prev - m_new)
    w_s = jnp.exp(ig - m_new)
    n_prev = n_ref[...]
    seg = seg_ref[...]
    segt = segt_ref[...]
    qk_dot = _dot(q * k, seg, HIGHEST)
    qn_dot = _dot(q * n_prev, seg, HIGHEST)
    s = qk_dot * w_s
    den = decay * qn_dot + s
    denom = jnp.maximum(jnp.abs(den), jnp.exp(-m_new))
    decay_x = _dot(decay, segt, HIGHEST)
    ws_x = _dot(w_s, segt, HIGHEST)
    qw_ref[...] = q * decay_x
    kw = k * ws_x
    kw_ref[...] = kw
    nnew_ref[...] = decay_x * n_prev + kw
    scal_ref[0] = decay
    scal_ref[1] = s
    scal_ref[2] = denom
    scal_ref[3] = m_new


def _sample_state_kernel(c_ref, qw_ref, kw_ref, v_ref, scal_ref, sel_ref, cnew_ref, h_ref):
    bb = qw_ref.shape[0]
    sel = sel_ref[...]
    q_cols = _dot_tn(qw_ref[...], sel, HIGHEST)
    k_cols = _dot_tn(kw_ref[...], sel, HIGHEST)
    decay = _dot_tn(scal_ref[0], sel, HIGHEST)
    s_qk = _dot_tn(scal_ref[1], sel, HIGHEST)
    denom = _dot_tn(scal_ref[2], sel, HIGHEST)
    for b in range(bb):
        lanes = slice(b * LANES, (b + 1) * LANES)
        for h in range(N_HEADS):
            rows = slice(h * D_K, (h + 1) * D_K)
            c = c_ref[b, h]
            v_row = v_ref[b:b + 1, h * D_V:(h + 1) * D_V]
            cnew_ref[b, h] = decay[h:h + 1, lanes] * c + k_cols[rows, lanes] * v_row
            num = jnp.sum(q_cols[rows, lanes] * c, axis=0, keepdims=True) + s_qk[h:h + 1, lanes] * v_row
            h_ref[b:b + 1, h * D_V:(h + 1) * D_V] = num / denom[h:h + 1, lanes]


def _sample_post_kernel(x_ref, hm_ref, rest_ref,
                        wsp_ref, bsp_ref, g_mh_ref, w_a_ref, g_vn_ref, w_b_ref, w_out_ref, g_ffn_ref, w_rt_ref,
                        b_r_ref, x1_ref, h2_ref, vg_ref, logits_ref):
    z = [rest_ref[:, p * 1024:(p + 1) * 1024] for p in range(5)]
    spatial = lambda vg: vg * wsp_ref[...] + bsp_ref[...]
    x1, vg = _post_mixer(x_ref[...], hm_ref[...], z[0], z[1], z[2], z[3], z[4], spatial,
                         g_mh_ref, w_a_ref, g_vn_ref, w_b_ref, w_out_ref)
    h2, logits = _ffn_input(x1, g_ffn_ref, w_rt_ref, b_r_ref)
    x1_ref[...] = x1
    _store_rows(h2_ref, h2)
    vg_ref[...] = vg
    logits_ref[...] = logits


def _sample_mixer(x_sample, state_conv, state_c, state_n, state_m, wts):
    ns, _, d = x_sample.shape
    xs = x_sample.reshape(ns, d)
    c_taps = [state_conv[:, t, :] for t in range(CONV_TAPS - 1)]
    n_rows = state_n.reshape(ns, D_QK)
    pre_out = (
        jax.ShapeDtypeStruct((ns, D_QK), F32),
        jax.ShapeDtypeStruct((ns, D_QK), F32),
        jax.ShapeDtypeStruct((ns, D_VAL), F32),
        jax.ShapeDtypeStruct((4, ns, LANES), F32),
        jax.ShapeDtypeStruct((ns, D_QK), F32),
        jax.ShapeDtypeStruct((ns, 2 * D_QK), F32),
        jax.ShapeDtypeStruct((ns, 5 * 1024), F32),
    )
    qw, kw, v, scal, n_new, z_qk, rest = pl.pallas_call(
        _sample_pre_kernel,
        out_shape=pre_out,
        in_specs=[_whole_vmem()] * 16,
        out_specs=tuple(_whole_vmem() for _ in pre_out),
        compiler_params=pltpu.CompilerParams(vmem_limit_bytes=VMEM_LIMIT),
        name="sample_pre",
    )(xs, c_taps[0], c_taps[1], c_taps[2], n_rows, jnp.pad(state_m, ((0, 0), (0, LANES - N_HEADS))),
      wts["g_mix"], wts["w_main"], wts["w_ig"], wts["w_fg"], wts["b_ig"], wts["b_fg"], wts["conv_w"],
      wts["conv_b"], wts["seg"], wts["segt"])
    bb = SAMPLE_STATE_BLOCK
    sel = (jnp.arange(bb)[:, None] == jnp.arange(bb * LANES)[None, :] // LANES).astype(F32)
    rows2 = lambda width: pl.BlockSpec((bb, width), lambda i: (i, 0))
    c_blk = pl.BlockSpec((bb, N_HEADS, D_K, D_V), lambda i: (i, 0, 0, 0))
    c_new, hm = pl.pallas_call(
        _sample_state_kernel,
        grid=(ns // bb,),
        in_specs=[c_blk, rows2(D_QK), rows2(D_QK), rows2(D_VAL),
                  pl.BlockSpec((4, bb, LANES), lambda i: (0, i, 0)),
                  pl.BlockSpec((bb, bb * LANES), lambda i: (0, 0))],
        out_specs=(c_blk, rows2(D_VAL)),
        out_shape=(jax.ShapeDtypeStruct((ns, N_HEADS, D_K, D_V), F32), jax.ShapeDtypeStruct((ns, D_VAL), F32)),
        compiler_params=pltpu.CompilerParams(dimension_semantics=("arbitrary",), vmem_limit_bytes=VMEM_LIMIT),
        name="sample_state",
    )(state_c, qw, kw, v, scal, sel)
    scal = scal[:, :, :N_HEADS]

    post_out = (jax.ShapeDtypeStruct((ns, d), F32), jax.ShapeDtypeStruct((ns * ROW_TILES, LANES), F32),
                jax.ShapeDtypeStruct((ns, d), F32), jax.ShapeDtypeStruct((N_EXPERTS, ns), F32))
    x1_s, h2_s, vg, logits_s = pl.pallas_call(
        _sample_post_kernel,
        out_shape=post_out,
        in_specs=[_whole_vmem()] * 13,
        out_specs=tuple(_whole_vmem() for _ in post_out),
        compiler_params=pltpu.CompilerParams(vmem_limit_bytes=VMEM_LIMIT),
        name="sample_post",
    )(xs, hm, rest, wts["wsp_row"], wts["bsp_row"], wts["g_mh"], wts["w_a"], wts["g_vn"],
      wts["w_b"], wts["w_out"], wts["g_ffn"], wts["w_rt"], wts["b_r_col"])
    s_conv = jnp.concatenate([state_conv[:, 1:, :], z_qk[:, None, :]], axis=1)
    return x1_s, h2_s, logits_s, s_conv, c_new, n_new.reshape(ns, N_HEADS, D_K), scal[3], vg


def _router_kernel(logits_ref, pos_ref, wt_ref, cnt_ref, run_ref, off_ref):
    phase = pl.program_id(0)
    i = pl.program_id(1)
    tb = logits_ref.shape[1]

    @pl.when(jnp.logical_and(phase == 0, i == 0))
    def _():
        cnt_ref[...] = jnp.zeros_like(cnt_ref)

    logits = logits_ref[...]
    e_iota = lax.broadcasted_iota(I32, logits.shape, 0)
    sels, vals = [], []
    work = logits
    for _ in range(TOP_K):
        mx = jnp.max(work, axis=0, keepdims=True)
        idx = jnp.min(jnp.where(work == mx, e_iota, N_EXPERTS), axis=0, keepdims=True)
        sel = e_iota == idx
        work = jnp.where(sel, -jnp.inf, work)
        sels.append(sel.astype(F32))
        vals.append(mx)
    sel_all = sels[0] + sels[1] + sels[2] + sels[3]
    block_count = jnp.sum(sel_all, axis=1, keepdims=True)

    @pl.when(phase == 0)
    def _():
        cnt_ref[...] += jnp.broadcast_to(block_count, cnt_ref.shape)

    @pl.when(jnp.logical_and(phase == 1, i == 0))
    def _():
        cnt = cnt_ref[...]
        padded = jnp.ceil(cnt * (1.0 / EXPERT_TILE)) * EXPERT_TILE
        r = lax.broadcasted_iota(I32, (N_EXPERTS, N_EXPERTS), 0)
        c = lax.broadcasted_iota(I32, (N_EXPERTS, N_EXPERTS), 1)
        off_ref[...] = _dot((c < r).astype(F32), padded, HIGHEST)
        run_ref[...] = jnp.zeros_like(run_ref)

    @pl.when(phase == 1)
    def _():
        r = lax.broadcasted_iota(I32, (tb, tb), 0)
        c = lax.broadcasted_iota(I32, (tb, tb), 1)
        before = _dot(sel_all.astype(BF16), (r < c).astype(BF16))
        row = before + run_ref[:, 0:1] + off_ref[:, 0:1]
        es = [jnp.exp(vv - vals[0]) for vv in vals]
        tot = es[0] + es[1] + es[2] + es[3]
        for kk in range(TOP_K):
            pos_ref[kk:kk + 1, :] = jnp.sum(sels[kk] * row, axis=0, keepdims=True).astype(I32)
            wt_ref[kk:kk + 1, :] = es[kk] / tot
        run_ref[...] += jnp.broadcast_to(block_count, run_ref.shape)


def _router(logits):
    n = logits.shape[1]
    tb = max(t for t in range(LANES, ROUTER_BLOCK_MAX + 1, LANES) if n % t == 0)
    nb = n // tb
    return pl.pallas_call(
        _router_kernel,
        grid=(2, nb),
        in_specs=[pl.BlockSpec((N_EXPERTS, tb), lambda p, i: (0, i))],
        out_specs=(pl.BlockSpec((TOP_K, tb), lambda p, i: (0, i * p)),
                   pl.BlockSpec((TOP_K, tb), lambda p, i: (0, i * p)),
                   pl.BlockSpec((N_EXPERTS, LANES), lambda p, i: (0, 0))),
        out_shape=(jax.ShapeDtypeStruct((TOP_K, n), I32), jax.ShapeDtypeStruct((TOP_K, n), F32),
                   jax.ShapeDtypeStruct((N_EXPERTS, LANES), F32)),
        scratch_shapes=[pltpu.VMEM((N_EXPERTS, LANES), F32), pltpu.VMEM((N_EXPERTS, LANES), F32)],
        compiler_params=pltpu.CompilerParams(
            dimension_semantics=("arbitrary", "arbitrary"), vmem_limit_bytes=VMEM_LIMIT),
        name="router",
    )(logits)


def _row_dma_loops(n_tokens, make_copy):
    def each(t0, fn):
        for u in range(ROW_DMA_UNROLL):
            for kk in range(TOP_K):
                fn(make_copy(t0 * ROW_DMA_UNROLL + u, kk), kk)

    def start(t0, carry):
        each(t0, lambda c, kk: c.start(priority=kk % 2))
        return carry

    def wait(t0, carry):
        each(t0, lambda c, kk: c.wait())
        return carry

    lax.fori_loop(0, n_tokens // ROW_DMA_UNROLL, start, 0)
    lax.fori_loop(0, n_tokens // ROW_DMA_UNROLL, wait, 0)


def _dispatch_kernel(clear_ref, pos_ref, h2_ref, xs_ref, zero_ref, sem, zsem):
    i = pl.program_id(0)
    tb = pos_ref.shape[1]
    tile_rows = EXPERT_TILE * ROW_TILES
    n_tiles = xs_ref.shape[0] // tile_rows

    @pl.when(i == 0)
    def _():
        zero_ref[...] = jnp.zeros_like(zero_ref)

        def tile_copy(t):
            start_row = pl.multiple_of(t * tile_rows, tile_rows)
            return pltpu.make_async_copy(zero_ref, xs_ref.at[pl.ds(start_row, tile_rows)], zsem)

        def start(t, carry):
            @pl.when(clear_ref[t] > 0)
            def _():
                tile_copy(t).start()
            return carry

        def wait(t, carry):
            @pl.when(clear_ref[t] > 0)
            def _():
                tile_copy(t).wait()
            return carry

        lax.fori_loop(0, n_tiles, start, 0)
        lax.fori_loop(0, n_tiles, wait, 0)

    _row_dma_loops(tb, lambda t, kk: pltpu.make_async_copy(
        _row_tile(h2_ref, t), _row_tile(xs_ref, pos_ref[kk, t]), sem))


def _dispatch(h2_all, pos, clear, n_rows):
    n = h2_all.shape[0] // ROW_TILES
    tb = TOKEN_BLOCK
    return pl.pallas_call(
        _dispatch_kernel,
        grid_spec=pltpu.PrefetchScalarGridSpec(
            num_scalar_prefetch=1,
            grid=(n // tb,),
            in_specs=[pl.BlockSpec((TOP_K, tb), lambda i, clr: (0, i), memory_space=pltpu.SMEM),
                      pl.BlockSpec((tb * ROW_TILES, LANES), lambda i, clr: (i, 0))],
            out_specs=pl.BlockSpec(memory_space=pl.ANY),
            scratch_shapes=[pltpu.VMEM((EXPERT_TILE * ROW_TILES, LANES), F32), pltpu.SemaphoreType.DMA,
                            pltpu.SemaphoreType.DMA],
        ),
        out_shape=jax.ShapeDtypeStruct((n_rows * ROW_TILES, LANES), F32),
        compiler_params=pltpu.CompilerParams(dimension_semantics=("arbitrary",), vmem_limit_bytes=VMEM_LIMIT),
        name="dispatch",
    )(clear, pos, h2_all)


def _expert_kernel(tile_e_ref, nvalid_ref, xs_ref, wg_ref, bg_ref, wu_ref, bu_ref, wd_ref, bd_ref, ys_ref,
                   wg_b, wu_b, wd_b):
    i = pl.program_id(0)
    prev = tile_e_ref[jnp.maximum(i - 1, 0)]
    fresh = jnp.logical_or(i == 0, tile_e_ref[i] != prev)

    @pl.when(jnp.logical_and(i < nvalid_ref[0], fresh))
    def _():
        wg_b[...] = wg_ref[0].astype(BF16)
        wu_b[...] = wu_ref[0].astype(BF16)
        wd_b[...] = wd_ref[0].astype(BF16)

    @pl.when(i < nvalid_ref[0])
    def _():
        x = _load_rows(xs_ref, EXPERT_TILE).astype(BF16)
        gate = jnp.minimum(_dot(x, wg_b[...]) + bg_ref[0], SWIGLU_LIMIT)
        up = jnp.clip(_dot(x, wu_b[...]) + bu_ref[0], -SWIGLU_LIMIT, SWIGLU_LIMIT)
        glu = gate * _sigmoid(SWIGLU_ALPHA * gate)
        _store_rows(ys_ref, _dot(((up + 1.0) * glu).astype(BF16), wd_b[...]) + bd_ref[0])

    @pl.when(i >= nvalid_ref[0])
    def _():
        ys_ref[...] = jnp.zeros_like(ys_ref)


def _experts(xs, tile_expert, n_valid, w_gate, b_gate, w_up, b_up, w_down, b_down):
    d, d_ff = w_gate.shape[1], w_gate.shape[2]
    tm = EXPERT_TILE
    n_tiles = xs.shape[0] // (tm * ROW_TILES)
    row_map = lambda i, te, nv: (jnp.minimum(i, nv[0] - 1), 0)
    w_map = lambda i, te, nv: (te[i], 0, 0)
    return pl.pallas_call(
        _expert_kernel,
        grid_spec=pltpu.PrefetchScalarGridSpec(
            num_scalar_prefetch=2,
            grid=(n_tiles,),
            in_specs=[pl.BlockSpec((tm * ROW_TILES, LANES), row_map),
                      pl.BlockSpec((1, d, d_ff), w_map), pl.BlockSpec((1, 1, d_ff), w_map),
                      pl.BlockSpec((1, d, d_ff), w_map), pl.BlockSpec((1, 1, d_ff), w_map),
                      pl.BlockSpec((1, d_ff, d), w_map), pl.BlockSpec((1, 1, d), w_map)],
            out_specs=pl.BlockSpec((tm * ROW_TILES, LANES), lambda i, te, nv: (i, 0)),
            scratch_shapes=[pltpu.VMEM((d, d_ff), BF16), pltpu.VMEM((d, d_ff), BF16), pltpu.VMEM((d_ff, d), BF16)],
        ),
        out_shape=jax.ShapeDtypeStruct(xs.shape, F32),
        compiler_params=pltpu.CompilerParams(dimension_semantics=("arbitrary",), vmem_limit_bytes=VMEM_LIMIT),
        name="experts",
    )(tile_expert, n_valid, xs, w_gate, b_gate[:, None, :], w_up, b_up[:, None, :], w_down, b_down[:, None, :])


def _combine_kernel(pos_ref, x1_ref, wt_ref, g_final_ref, ys_ref, yp_ref, ysmp_ref, buf_ref, sem):
    i = pl.program_id(0)
    last = pl.num_programs(0) - 1
    tb = x1_ref.shape[0]
    _row_dma_loops(tb, lambda t, kk: pltpu.make_async_copy(
        _row_tile(ys_ref, pos_ref[kk, t]), _row_tile(buf_ref.at[kk], t), sem))
    out = x1_ref[...]
    for kk in range(TOP_K):
        out = out + wt_ref[:, kk:kk + 1] * _load_rows(buf_ref.at[kk], tb)
    y = _rms(out, g_final_ref[...])

    @pl.when(i < last)
    def _():
        yp_ref[0] = y

    @pl.when(i == last)
    def _():
        ysmp_ref[...] = y[0:ysmp_ref.shape[0], :]


def _combine(x1_all, pos, wt_cols, ys, g_final_row, batch, seq, ns):
    d = x1_all.shape[1]
    tb = TOKEN_BLOCK
    nt = seq // tb
    n_prompt_blocks = batch * nt
    nb = n_prompt_blocks + 1

    def prompt_block_index(i):
        i = jnp.minimum(i, n_prompt_blocks - 1)
        step, s = i // SEQS_PER_STEP, i % SEQS_PER_STEP
        return ((step // nt) * SEQS_PER_STEP + s, step % nt, 0)

    return pl.pallas_call(
        _combine_kernel,
        grid=(nb,),
        in_specs=[pl.BlockSpec((TOP_K, tb), lambda i: (0, i), memory_space=pltpu.SMEM),
                  pl.BlockSpec((tb, d), lambda i: (i, 0)),
                  pl.BlockSpec((tb, TOP_K), lambda i: (i, 0)),
                  pl.BlockSpec((1, d), lambda i: (0, 0)),
                  pl.BlockSpec(memory_space=pl.ANY)],
        out_specs=(pl.BlockSpec((1, tb, d), prompt_block_index),
                   pl.BlockSpec((ns, d), lambda i: (0, 0))),
        out_shape=(jax.ShapeDtypeStruct((batch, seq, d), F32), jax.ShapeDtypeStruct((ns, d), F32)),
        scratch_shapes=[pltpu.VMEM((TOP_K, tb * ROW_TILES, LANES), F32), pltpu.SemaphoreType.DMA],
        compiler_params=pltpu.CompilerParams(dimension_semantics=("arbitrary",), vmem_limit_bytes=VMEM_LIMIT),
        name="combine",
    )(pos, x1_all, wt_cols, g_final_row, ys)


def _prepare_weights(g_mix, w_in, conv_w, conv_b, b_igate, b_fgate, g_mhnorm, w_a, w_spatial, b_spatial, g_vnorm,
                     w_b, w_out, g_ffn, w_router, b_router):
    d = w_in.shape[0]
    gate_lo = 2 * D_QK + D_VAL
    gate_hi = gate_lo + GATE_COLS
    row = lambda a: a.reshape(1, -1).astype(F32)
    w_if = w_in[:, gate_lo:gate_hi]
    gate_b = jnp.concatenate([b_igate, b_fgate])
    tril = jnp.tril(jnp.ones((SPATIAL_CHUNK, SPATIAL_CHUNK), w_spatial.dtype))
    head_of_lane = jnp.arange(D_QK) // D_K
    seg = (head_of_lane[:, None] == jnp.arange(LANES)[None, :]).astype(F32)
    pad_cols = lambda a: jnp.pad(a, ((0, 0), (0, LANES - N_HEADS)))
    return dict(
        g_mix=row(g_mix),
        w_main=jnp.concatenate([w_in[:, :gate_lo], w_in[:, gate_hi:]], axis=1).astype(BF16),
        w_ig=pad_cols(w_if[:, :N_HEADS]).astype(BF16),
        w_fg=pad_cols(w_if[:, N_HEADS:]).astype(BF16),
        w_ift=w_if.T.astype(BF16),
        b_ig=pad_cols(b_igate.reshape(1, N_HEADS)),
        b_fg=pad_cols(b_fgate.reshape(1, N_HEADS)),
        gate_b_col=jnp.broadcast_to(gate_b[:, None], (GATE_COLS, LANES)),
        conv_w=conv_w, conv_b=row(conv_b), g_mh=row(g_mhnorm), w_a=w_a.astype(BF16),
        ws_masked=(w_spatial * tril).astype(BF16),
        bsp_full=jnp.repeat(b_spatial.T, GROUP_DIM, axis=1),
        wsp_row=jnp.repeat(w_spatial[:, 0, 0], GROUP_DIM).reshape(1, -1),
        bsp_row=jnp.repeat(b_spatial[:, 0], GROUP_DIM).reshape(1, -1),
        g_vn=row(g_vnorm), w_b=w_b.astype(BF16), w_out=w_out.astype(BF16), g_ffn=row(g_ffn),
        w_rt=w_router.T, b_r_col=jnp.broadcast_to(b_router[:, None], (N_EXPERTS, LANES)),
        seg=seg, segt=seg.T,
    )


def _tile_metadata(counts, n_tiles):
    tiles_per_expert = (counts + EXPERT_TILE - 1) // EXPERT_TILE
    tile_end = jnp.cumsum(tiles_per_expert)
    n_valid = tile_end[-1]
    tile_ids = jnp.minimum(jnp.arange(n_tiles, dtype=I32), n_valid - 1)
    tile_expert = jnp.sum(tile_end[None, :] <= tile_ids[:, None], axis=1).astype(I32)
    tail_tile = jnp.where(counts > 0, tile_end - 1, -1)
    all_tiles = jnp.arange(n_tiles, dtype=I32)
    clear = jnp.logical_or(all_tiles >= n_valid, jnp.any(all_tiles[:, None] == tail_tile[None, :], axis=1))
    return tile_expert, n_valid.reshape(1).astype(I32), clear.astype(I32)


def kernel(x_prompt, x_sample, state_conv, state_C, state_n, state_m, g_mix, w_in, conv_w, conv_b, b_igate, b_fgate,
           g_mhnorm, w_a, w_spatial, b_spatial, g_vnorm, w_b, w_out, g_ffn, w_router, b_router, w_gate, b_gate,
           w_up, b_up, w_down, b_down, g_final):
    depth = g_mix.shape[0]
    assert depth == 1, "single-layer trunk"
    batch, seq, d = x_prompt.shape
    ns = x_sample.shape[0]
    assert seq % PROMPT_BLOCK == 0 and ns <= TOKEN_BLOCK and PROMPT_BLOCK == TOKEN_BLOCK
    assert batch % SEQS_PER_STEP == 0
    assert PROMPT_BLOCK % MLSTM_CHUNK == 0 and MLSTM_CHUNK % SPATIAL_CHUNK == 0
    wts = _prepare_weights(g_mix[0], w_in[0], conv_w[0], conv_b[0], b_igate[0], b_fgate[0], g_mhnorm[0], w_a[0],
                           w_spatial[0], b_spatial[0], g_vnorm[0], w_b[0], w_out[0], g_ffn[0], w_router[0],
                           b_router[0])

    x1_s, h2_s, logits_s, s_conv, s_c, s_n, s_m, vg = _sample_mixer(
        x_sample, state_conv[0], state_C[0], state_n[0], state_m[0], wts)
    x1_all, h2_all, logits, p_conv, p_c, p_n, p_m = _prompt_mixer(x_prompt, x1_s, h2_s, logits_s, wts)
    n_total = x1_all.shape[0]

    pos, wt, counts = _router(logits)
    n_tiles = -(-(n_total * TOP_K) // EXPERT_TILE) + N_EXPERTS
    tile_expert, n_valid, clear = _tile_metadata(counts[:, 0].astype(I32), n_tiles)
    xs = _dispatch(h2_all, pos, clear, n_tiles * EXPERT_TILE)
    ys = _experts(xs, tile_expert, n_valid, w_gate[0], b_gate[0], w_up[0], b_up[0], w_down[0], b_down[0])
    y_prompt, y_sample = _combine(x1_all, pos, wt.T, ys, g_final.reshape(1, d), batch, seq, ns)

    return (y_prompt, y_sample.reshape(ns, 1, d),
            p_conv[None], p_c[None], p_n[None], p_m[None, :, :, 0],
            s_conv[None], s_c[None], s_n[None], s_m[None], vg.reshape(1, ns, 1, N_GROUPS, GROUP_DIM))
```

```python
import functools

import jax
import jax.numpy as jnp
from jax import lax
from jax.experimental import pallas as pl
from jax.experimental.pallas import tpu as pltpu

F32 = jnp.float32
BF16 = jnp.bfloat16
I32 = jnp.int32
HIGHEST = lax.Precision.HIGHEST

N_HEADS = 8
D_K = 64
D_V = 128
D_QK = N_HEADS * D_K
D_VAL = N_HEADS * D_V
CONV_TAPS = 4
GATE_COLS = 2 * N_HEADS
N_GROUPS = 8
GROUP_DIM = 128
SPATIAL_CHUNK = 128
N_EXPERTS = 32
TOP_K = 4
SWIGLU_LIMIT = 7.0
SWIGLU_ALPHA = 1.702
RMS_EPS = 1e-6
EMPTY_MAX = -1e30
LANES = 128
SUBLANES = 8
D_MODEL = 1024
ROW_TILES = D_MODEL // LANES

PROMPT_BLOCK = 256
MLSTM_CHUNK = 256
SEQS_PER_STEP = 1
SAMPLE_STATE_BLOCK = 8
TOKEN_BLOCK = 256
ROW_DMA_UNROLL = 2
ROUTER_BLOCK_MAX = 1280
EXPERT_TILE = 512
VMEM_LIMIT = 56 * 1024 * 1024


def _dot(a, b, precision=None):
    return jnp.dot(a, b, preferred_element_type=F32, precision=precision)


def _dot_nt(a, b, precision=None):
    return lax.dot_general(a, b, (((1,), (1,)), ((), ())), preferred_element_type=F32, precision=precision)


def _dot_tn(a, b, precision=None):
    return lax.dot_general(a, b, (((0,), (0,)), ((), ())), preferred_element_type=F32, precision=precision)


def _rms(x, g):
    return x * lax.rsqrt(jnp.mean(x * x, axis=-1, keepdims=True) + RMS_EPS) * g


def _sigmoid(x):
    return jax.nn.sigmoid(x)


def _gelu(x):
    return 0.5 * x * (1.0 + lax.erf(x * (2.0 ** -0.5)))


def _log_sigmoid(x):
    return jax.nn.log_sigmoid(x)


def _store_rows(ref, val):
    rows = val.shape[0]
    for jt in range(ROW_TILES):
        ref[pl.ds(jt, rows, stride=ROW_TILES), :] = val[:, jt * LANES:(jt + 1) * LANES]


def _load_rows(ref, rows):
    return jnp.concatenate([ref[pl.ds(jt, rows, stride=ROW_TILES), :] for jt in range(ROW_TILES)], axis=1)


def _row_tile(ref, r):
    return ref.at[pl.ds(pl.multiple_of(r * ROW_TILES, ROW_TILES), ROW_TILES)]


def _in_projection(hn, w_main_ref, piece):
    ref, idx = (w_main_ref[0], piece) if piece < 2 else (w_main_ref[1], piece - 2)
    return _dot(hn, ref[:, idx * 1024:(idx + 1) * 1024])


def _post_mixer(x, hm, z_o, z_u, z_vg, z_ga, z_gb, spatial_fn,
                g_mh_ref, w_a_ref, g_vn_ref, w_b_ref, w_out_ref):
    parts = []
    for h in range(N_HEADS):
        hh = hm[:, h * D_V:(h + 1) * D_V]
        parts.append(hh * lax.rsqrt(jnp.mean(hh * hh, axis=-1, keepdims=True) + RMS_EPS))
    hmn = jnp.concatenate(parts, axis=1) * g_mh_ref[...]
    y_a = _dot((hmn * _sigmoid(z_o)).astype(BF16), w_a_ref[...])
    u = _gelu(z_u)
    vg = _rms(_gelu(z_vg), g_vn_ref[...])
    s = spatial_fn(vg)
    y_b = _dot((u * s).astype(BF16), w_b_ref[...])
    merged = _sigmoid(z_ga) * y_a + _sigmoid(z_gb) * y_b
    x1 = x + _dot(merged.astype(BF16), w_out_ref[...])
    return x1, vg


def _ffn_input(x1, g_ffn_ref, w_rt_ref, b_r_ref):
    h2 = _rms(x1, g_ffn_ref[...])
    return h2, _dot_nt(w_rt_ref[...], h2, HIGHEST) + b_r_ref[:, 0:1]


def _prompt_mixer_kernel(x_ref, x1s_ref, h2s_ref, logits_s_ref, *refs, blocks_per_seq, n_prompt_steps):
    g = pl.program_id(0)
    block_weights = refs[:16]
    g_ffn_ref, w_rt_ref, b_r_ref = refs[16:19]
    x1_ref, h2_ref, logits_ref, pconv_ref, pc_ref, pn_ref, pm_ref, xp_ref, x1_keep_ref = refs[19:]
    tb = x_ref.shape[1]

    def finish_previous_block():
        for s in range(SEQS_PER_STEP):
            h2, logits = _ffn_input(x1_keep_ref[s], g_ffn_ref, w_rt_ref, b_r_ref)
            _store_rows(h2_ref.at[pl.ds(s * tb * ROW_TILES, tb * ROW_TILES)], h2)
            logits_ref[:, s * tb:(s + 1) * tb] = logits

    @pl.when(g == 0)
    def _():
        x1_keep_ref[...] = jnp.zeros_like(x1_keep_ref)

    @pl.when(jnp.logical_and(g < n_prompt_steps, g % blocks_per_seq == 0))
    def _():
        pc_ref[...] = jnp.zeros_like(pc_ref)
        pn_ref[...] = jnp.zeros_like(pn_ref)
        pm_ref[...] = jnp.full(pm_ref.shape, EMPTY_MAX, F32)
        xp_ref[:, 0:SUBLANES, :] = jnp.zeros((xp_ref.shape[0], SUBLANES, xp_ref.shape[2]), F32)

    @pl.when(g < n_prompt_steps)
    def _():
        finish_previous_block()
        for s in range(SEQS_PER_STEP):
            _prompt_block(x_ref.at[s], *block_weights, x1_ref.at[pl.ds(s * tb, tb)],
                          pconv_ref.at[s], pc_ref.at[s], pn_ref.at[s], pm_ref.at[s], xp_ref.at[s], x1_keep_ref.at[s])

    ns = x1s_ref.shape[0]
    n_pad = x1_ref.shape[0] - ns

    @pl.when(g == n_prompt_steps)
    def _():
        finish_previous_block()
        x1_ref[0:ns, :] = x1s_ref[...]
        x1_ref[ns:, :] = jnp.zeros((n_pad, x1_ref.shape[1]), F32)

    @pl.when(g == n_prompt_steps + 1)
    def _():
        h2_ref[0:ns * ROW_TILES, :] = h2s_ref[...]
        h2_ref[ns * ROW_TILES:, :] = jnp.zeros((n_pad * ROW_TILES, LANES), F32)
        logits_ref[:, 0:ns] = logits_s_ref[...]
        logits_ref[:, ns:] = jnp.broadcast_to(b_r_ref[:, 0:1], (N_EXPERTS, n_pad))


def _prompt_block(x_ref, g_mix_ref, w_lo_ref, w_hi_ref, w_gates_ref, w_ift_ref, b_gates_ref,
                  gate_b_col_ref, conv_w_ref, conv_b_ref, g_mh_ref, w_a_ref, ws_ref, bsp_ref, g_vn_ref,
                  w_b_ref, w_out_ref,
                  x1_ref, pconv_ref, pc_ref, pn_ref, pm_ref, xp_ref, x1_keep_ref):
    w_main_ref = (w_lo_ref, w_hi_ref)
    tb = x_ref.shape[0]
    x = x_ref[...]
    hn = _rms(x, g_mix_ref[...]).astype(BF16)

    z_qk = _in_projection(hn, w_main_ref, 0)
    xp_ref[SUBLANES:SUBLANES + tb, :] = z_qk
    y = conv_b_ref[...] + conv_w_ref[CONV_TAPS - 1:CONV_TAPS, :] * z_qk
    for tap in range(CONV_TAPS - 1):
        back = CONV_TAPS - 1 - tap
        y = y + conv_w_ref[tap:tap + 1, :] * xp_ref[SUBLANES - back:SUBLANES - back + tb, :]
    tail = z_qk[tb - (CONV_TAPS - 1):tb, :]
    xp_ref[SUBLANES - (CONV_TAPS - 1):SUBLANES, :] = tail
    pconv_ref[...] = tail
    qk = y * _sigmoid(y)
    q = qk[:, :D_QK]
    k = qk[:, D_QK:] * (D_K ** -0.5)
    qb = q.astype(BF16)
    v = _in_projection(hn, w_main_ref, 1)
    vb = v.astype(BF16)

    ig_c = _dot(hn, w_gates_ref[...]) + b_gates_ref[...]
    lf_c = _log_sigmoid(ig_c)
    gr = _dot_nt(w_ift_ref[...], hn) + gate_b_col_ref[:, 0:1]
    ig_r = gr[0:N_HEADS, :]
    lf_r = _log_sigmoid(gr[N_HEADS:GATE_COLS, :])
    ch = MLSTM_CHUNK
    row_i = lax.broadcasted_iota(I32, (ch, ch), 0)
    col_i = lax.broadcasted_iota(I32, (ch, ch), 1)
    causal = col_i <= row_i
    tri = causal.astype(F32)

    hm_chunks = []
    for c0 in range(0, tb, ch):
        rows = slice(c0, c0 + ch)
        b_c = _dot(tri, lf_c[rows, :], HIGHEST)
        b_r = _dot_nt(lf_r[:, rows], tri, HIGHEST)
        hm_parts = []
        for h in range(N_HEADS):
            bt = b_c[:, N_HEADS + h:N_HEADS + h + 1]
            ig_col = ig_c[rows, h:h + 1]
            bs = b_r[h:h + 1, :]
            ig_row = ig_r[h:h + 1, rows]
            m_prev = pm_ref[h:h + 1, 0:1]
            c_prev = pc_ref[h]
            n_prev = pn_ref[h:h + 1, :]
            q_h = q[rows, h * D_K:(h + 1) * D_K]
            qb_h = qb[rows, h * D_K:(h + 1) * D_K]
            k_h = k[rows, h * D_K:(h + 1) * D_K]
            vb_h = vb[rows, h * D_V:(h + 1) * D_V]

            dmat = jnp.where(causal, bt - bs + ig_row, -jnp.inf)
            inter = bt + m_prev
            m_t = jnp.maximum(inter, jnp.max(dmat, axis=-1, keepdims=True))
            w_inter = jnp.exp(inter - m_t)
            s = _dot_nt(qb_h, k_h.astype(BF16)) * jnp.exp(dmat - m_t)
            num = w_inter * _dot(qb_h, c_prev.astype(BF16)) + _dot(s.astype(BF16), vb_h)
            den = w_inter * jnp.sum(q_h * n_prev, axis=-1, keepdims=True) + jnp.sum(s, axis=-1, keepdims=True)
            hm_parts.append(num / jnp.maximum(jnp.abs(den), jnp.exp(-m_t)))

            m_new = m_t[ch - 1:ch, :]
            b_last = bt[ch - 1:ch, :]
            decay = jnp.exp(b_last + m_prev - m_new)
            w_s = jnp.exp(b_last - bt + ig_col - m_new)
            kw = k_h * w_s
            pc_ref[h] = decay * c_prev + _dot_tn(kw.astype(BF16), vb_h)
            pn_ref[h:h + 1, :] = decay * n_prev + jnp.sum(kw, axis=0, keepdims=True)
            pm_ref[h:h + 1, :] = jnp.broadcast_to(m_new, (1, pm_ref.shape[1]))
        hm_chunks.append(jnp.concatenate(hm_parts, axis=1))
    hm = jnp.concatenate(hm_chunks, axis=0)

    def spatial(vg):
        vgb = vg.astype(BF16)
        blocks = [_dot(ws_ref[g], vgb[:, g * GROUP_DIM:(g + 1) * GROUP_DIM]) for g in range(N_GROUPS)]
        return jnp.concatenate(blocks, axis=1) + bsp_ref[...]

    z_o, z_u, z_vg, z_ga, z_gb = [_in_projection(hn, w_main_ref, 2 + p) for p in range(5)]
    x1, _ = _post_mixer(x, hm, z_o, z_u, z_vg, z_ga, z_gb, spatial, g_mh_ref, w_a_ref, g_vn_ref, w_b_ref, w_out_ref)
    x1_ref[...] = x1
    x1_keep_ref[...] = x1


def _whole_vmem():
    return pl.BlockSpec(memory_space=pltpu.VMEM)


def _prompt_mixer(x_prompt, x1_sample, h2_sample, logits_sample, wts):
    batch, seq, d = x_prompt.shape
    tb = PROMPT_BLOCK
    sps = SEQS_PER_STEP
    nt = seq // tb
    n_steps = (batch // sps) * nt
    step_rows = sps * tb
    n_total = (n_steps + 1) * step_rows
    out_shapes = (
        jax.ShapeDtypeStruct((n_total, d), F32),
        jax.ShapeDtypeStruct((n_total * ROW_TILES, LANES), F32),
        jax.ShapeDtypeStruct((N_EXPERTS, n_total), F32),
        jax.ShapeDtypeStruct((batch, CONV_TAPS - 1, 2 * D_QK), F32),
        jax.ShapeDtypeStruct((batch, N_HEADS, D_K, D_V), F32),
        jax.ShapeDtypeStruct((batch, N_HEADS, D_K), F32),
        jax.ShapeDtypeStruct((batch, N_HEADS, LANES), F32),
    )
    group_of = lambda g: jnp.minimum(g, n_steps - 1) // nt
    lagged = lambda g: jnp.clip(g - 1, 0, n_steps)
    out_specs = (
        pl.BlockSpec((step_rows, d), lambda g: (jnp.minimum(g, n_steps), 0)),
        pl.BlockSpec((step_rows * ROW_TILES, LANES), lambda g: (lagged(g), 0)),
        pl.BlockSpec((N_EXPERTS, step_rows), lambda g: (0, lagged(g))),
        pl.BlockSpec((sps, CONV_TAPS - 1, 2 * D_QK), lambda g: (group_of(g), 0, 0)),
        pl.BlockSpec((sps, N_HEADS, D_K, D_V), lambda g: (group_of(g), 0, 0, 0)),
        pl.BlockSpec((sps, N_HEADS, D_K), lambda g: (group_of(g), 0, 0)),
        pl.BlockSpec((sps, N_HEADS, LANES), lambda g: (group_of(g), 0, 0)),
    )
    in_specs = ([pl.BlockSpec((sps, tb, d), lambda g: (group_of(g), jnp.minimum(g, n_steps - 1) % nt, 0))]
                + [_whole_vmem()] * 22)
    return pl.pallas_call(
        functools.partial(_prompt_mixer_kernel, blocks_per_seq=nt, n_prompt_steps=n_steps),
        grid=(n_steps + 2,),
        in_specs=in_specs,
        out_specs=out_specs,
        out_shape=out_shapes,
        scratch_shapes=[pltpu.VMEM((sps, SUBLANES + tb, 2 * D_QK), F32), pltpu.VMEM((sps, tb, d), F32)],
        compiler_params=pltpu.CompilerParams(dimension_semantics=("arbitrary",), vmem_limit_bytes=VMEM_LIMIT),
        name="prompt_mixer",
    )(x_prompt, x1_sample, h2_sample, logits_sample,
      wts["g_mix"], wts["w_lo"], wts["w_hi"], wts["w_gates"], wts["w_ift"], wts["b_gates"],
      wts["gate_b_col"], wts["conv_w"], wts["conv_b"], wts["g_mh"], wts["w_a"], wts["ws_masked"], wts["bsp_full"],
      wts["g_vn"], wts["w_b"], wts["w_out"], wts["g_ffn"], wts["w_rt"], wts["b_r_col"])


def _sample_pre_kernel(x_ref, c0_ref, c1_ref, c2_ref, n_ref, m_ref,
                       g_mix_ref, w_lo_ref, w_hi_ref, w_ig_ref, w_fg_ref, b_ig_ref, b_fg_ref, conv_w_ref, conv_b_ref,
                       seg_ref, segt_ref,
                       qw_ref, kw_ref, v_ref, scal_ref, nnew_ref, zqk_ref, rest_ref):
    w_main_ref = (w_lo_ref, w_hi_ref)
    x = x_ref[...]
    hn = _rms(x, g_mix_ref[...]).astype(BF16)
    z_qk = _in_projection(hn, w_main_ref, 0)
    zqk_ref[...] = z_qk
    y = (conv_b_ref[...] + conv_w_ref[0:1, :] * c0_ref[...] + conv_w_ref[1:2, :] * c1_ref[...]
         + conv_w_ref[2:3, :] * c2_ref[...] + conv_w_ref[3:4, :] * z_qk)
    qk = y * _sigmoid(y)
    q = qk[:, :D_QK]
    k = qk[:, D_QK:] * (D_K ** -0.5)
    v_ref[...] = _in_projection(hn, w_main_ref, 1)
    for p in range(5):
        rest_ref[:, p * 1024:(p + 1) * 1024] = _in_projection(hn, w_main_ref, 2 + p)

    ig = _dot(hn, w_ig_ref[...]) + b_ig_ref[...]
    lf = _log_sigmoid(_dot(hn, w_fg_ref[...]) + b_fg_ref[...])
    m_prev = m_ref[...]
    m_new = jnp.maximum(lf + m_prev, ig)
    decay = jnp.exp(lf + m_prev - m_new)
    w_s = jnp.exp(ig - m_new)
    n_prev = n_ref[...]
    seg = seg_ref[...]
    segt = segt_ref[...]
    qk_dot = _dot(q * k, seg, HIGHEST)
    qn_dot = _dot(q * n_prev, seg, HIGHEST)
    s = qk_dot * w_s
    den = decay * qn_dot + s
    denom = jnp.maximum(jnp.abs(den), jnp.exp(-m_new))
    decay_x = _dot(decay, segt, HIGHEST)
    ws_x = _dot(w_s, segt, HIGHEST)
    qw_ref[...] = q * decay_x
    kw = k * ws_x
    kw_ref[...] = kw
    nnew_ref[...] = decay_x * n_prev + kw
    scal_ref[0] = decay
    scal_ref[1] = s
    scal_ref[2] = denom
    scal_ref[3] = m_new


def _sample_state_kernel(c_ref, qw_ref, kw_ref, v_ref, scal_ref, eye_ref, cnew_ref, h_ref):
    bb = qw_ref.shape[0]
    eye = eye_ref[...]
    q_t = _dot_tn(qw_ref[...], eye, HIGHEST)
    k_t = _dot_tn(kw_ref[...], eye, HIGHEST)
    decay = _dot_tn(scal_ref[0], eye, HIGHEST)
    s_qk = _dot_tn(scal_ref[1], eye, HIGHEST)
    denom = _dot_tn(scal_ref[2], eye, HIGHEST)
    for b in range(bb):
        for h in range(N_HEADS):
            rows = slice(h * D_K, (h + 1) * D_K)
            c = c_ref[b, h]
            v_row = v_ref[b:b + 1, h * D_V:(h + 1) * D_V]
            cnew_ref[b, h] = decay[h:h + 1, b:b + 1] * c + k_t[rows, b:b + 1] * v_row
            num = jnp.sum(q_t[rows, b:b + 1] * c, axis=0, keepdims=True) + s_qk[h:h + 1, b:b + 1] * v_row
            h_ref[b:b + 1, h * D_V:(h + 1) * D_V] = num / denom[h:h + 1, b:b + 1]


def _sample_post_kernel(x_ref, hm_ref, rest_ref,
                        wsp_ref, bsp_ref, g_mh_ref, w_a_ref, g_vn_ref, w_b_ref, w_out_ref, g_ffn_ref, w_rt_ref,
                        b_r_ref, x1_ref, h2_ref, vg_ref, logits_ref):
    z = [rest_ref[:, p * 1024:(p + 1) * 1024] for p in range(5)]
    spatial = lambda vg: vg * wsp_ref[...] + bsp_ref[...]
    x1, vg = _post_mixer(x_ref[...], hm_ref[...], z[0], z[1], z[2], z[3], z[4], spatial,
                         g_mh_ref, w_a_ref, g_vn_ref, w_b_ref, w_out_ref)
    h2, logits = _ffn_input(x1, g_ffn_ref, w_rt_ref, b_r_ref)
    x1_ref[...] = x1
    _store_rows(h2_ref, h2)
    vg_ref[...] = vg
    logits_ref[...] = logits


def _sample_mixer(x_sample, state_conv, state_c, state_n, state_m, wts):
    ns, _, d = x_sample.shape
    xs = x_sample.reshape(ns, d)
    c_taps = [state_conv[:, t, :] for t in range(CONV_TAPS - 1)]
    n_rows = state_n.reshape(ns, D_QK)
    pre_out = (
        jax.ShapeDtypeStruct((ns, D_QK), F32),
        jax.ShapeDtypeStruct((ns, D_QK), F32),
        jax.ShapeDtypeStruct((ns, D_VAL), F32),
        jax.ShapeDtypeStruct((4, ns, LANES), F32),
        jax.ShapeDtypeStruct((ns, D_QK), F32),
        jax.ShapeDtypeStruct((ns, 2 * D_QK), F32),
        jax.ShapeDtypeStruct((ns, 5 * 1024), F32),
    )
    qw, kw, v, scal, n_new, z_qk, rest = pl.pallas_call(
        _sample_pre_kernel,
        out_shape=pre_out,
        in_specs=[_whole_vmem()] * 17,
        out_specs=tuple(_whole_vmem() for _ in pre_out),
        compiler_params=pltpu.CompilerParams(vmem_limit_bytes=VMEM_LIMIT),
        name="sample_pre",
    )(xs, c_taps[0], c_taps[1], c_taps[2], n_rows, jnp.pad(state_m, ((0, 0), (0, LANES - N_HEADS))),
      wts["g_mix"], wts["w_lo"], wts["w_hi"], wts["w_ig"], wts["w_fg"], wts["b_ig"], wts["b_fg"], wts["conv_w"],
      wts["conv_b"], wts["seg"], wts["segt"])
    bb = SAMPLE_STATE_BLOCK
    eye = (jnp.arange(bb)[:, None] == jnp.arange(LANES)[None, :]).astype(F32)
    rows2 = lambda width: pl.BlockSpec((bb, width), lambda i: (i, 0))
    c_blk = pl.BlockSpec((bb, N_HEADS, D_K, D_V), lambda i: (i, 0, 0, 0))
    c_new, hm = pl.pallas_call(
        _sample_state_kernel,
        grid=(ns // bb,),
        in_specs=[c_blk, rows2(D_QK), rows2(D_QK), rows2(D_VAL),
                  pl.BlockSpec((4, bb, LANES), lambda i: (0, i, 0)),
                  pl.BlockSpec((bb, LANES), lambda i: (0, 0))],
        out_specs=(c_blk, rows2(D_VAL)),
        out_shape=(jax.ShapeDtypeStruct((ns, N_HEADS, D_K, D_V), F32), jax.ShapeDtypeStruct((ns, D_VAL), F32)),
        compiler_params=pltpu.CompilerParams(dimension_semantics=("arbitrary",), vmem_limit_bytes=VMEM_LIMIT),
        name="sample_state",
    )(state_c, qw, kw, v, scal, eye)
    scal = scal[:, :, :N_HEADS]

    post_out = (jax.ShapeDtypeStruct((ns, d), F32), jax.ShapeDtypeStruct((ns * ROW_TILES, LANES), F32),
                jax.ShapeDtypeStruct((ns, d), F32), jax.ShapeDtypeStruct((N_EXPERTS, ns), F32))
    x1_s, h2_s, vg, logits_s = pl.pallas_call(
        _sample_post_kernel,
        out_shape=post_out,
        in_specs=[_whole_vmem()] * 13,
        out_specs=tuple(_whole_vmem() for _ in post_out),
        compiler_params=pltpu.CompilerParams(vmem_limit_bytes=VMEM_LIMIT),
        name="sample_post",
    )(xs, hm, rest, wts["wsp_row"], wts["bsp_row"], wts["g_mh"], wts["w_a"], wts["g_vn"],
      wts["w_b"], wts["w_out"], wts["g_ffn"], wts["w_rt"], wts["b_r_col"])
    s_conv = jnp.concatenate([state_conv[:, 1:, :], z_qk[:, None, :]], axis=1)
    return x1_s, h2_s, logits_s, s_conv, c_new, n_new.reshape(ns, N_HEADS, D_K), scal[3], vg


def _router_kernel(logits_ref, pos_ref, wt_ref, cnt_ref, run_ref, off_ref):
    phase = pl.program_id(0)
    i = pl.program_id(1)
    tb = logits_ref.shape[1]

    @pl.when(jnp.logical_and(phase == 0, i == 0))
    def _():
        cnt_ref[...] = jnp.zeros_like(cnt_ref)

    logits = logits_ref[...]
    e_iota = lax.broadcasted_iota(I32, logits.shape, 0)
    sels, vals = [], []
    work = logits
    for _ in range(TOP_K):
        mx = jnp.max(work, axis=0, keepdims=True)
        idx = jnp.min(jnp.where(work == mx, e_iota, N_EXPERTS), axis=0, keepdims=True)
        sel = e_iota == idx
        work = jnp.where(sel, -jnp.inf, work)
        sels.append(sel.astype(F32))
        vals.append(mx)
    sel_all = sels[0] + sels[1] + sels[2] + sels[3]
    block_count = jnp.sum(sel_all, axis=1, keepdims=True)

    @pl.when(phase == 0)
    def _():
        cnt_ref[...] += jnp.broadcast_to(block_count, cnt_ref.shape)

    @pl.when(jnp.logical_and(phase == 1, i == 0))
    def _():
        cnt = cnt_ref[...]
        padded = jnp.ceil(cnt * (1.0 / EXPERT_TILE)) * EXPERT_TILE
        r = lax.broadcasted_iota(I32, (N_EXPERTS, N_EXPERTS), 0)
        c = lax.broadcasted_iota(I32, (N_EXPERTS, N_EXPERTS), 1)
        off_ref[...] = _dot((c < r).astype(F32), padded, HIGHEST)
        run_ref[...] = jnp.zeros_like(run_ref)

    @pl.when(phase == 1)
    def _():
        r = lax.broadcasted_iota(I32, (tb, tb), 0)
        c = lax.broadcasted_iota(I32, (tb, tb), 1)
        before = _dot(sel_all.astype(BF16), (r < c).astype(BF16))
        row = before + run_ref[:, 0:1] + off_ref[:, 0:1]
        es = [jnp.exp(vv - vals[0]) for vv in vals]
        tot = es[0] + es[1] + es[2] + es[3]
        for kk in range(TOP_K):
            pos_ref[kk:kk + 1, :] = jnp.sum(sels[kk] * row, axis=0, keepdims=True).astype(I32)
            wt_ref[kk:kk + 1, :] = es[kk] / tot
        run_ref[...] += jnp.broadcast_to(block_count, run_ref.shape)


def _router(logits):
    n = logits.shape[1]
    tb = max(t for t in range(LANES, ROUTER_BLOCK_MAX + 1, LANES) if n % t == 0)
    nb = n // tb
    return pl.pallas_call(
        _router_kernel,
        grid=(2, nb),
        in_specs=[pl.BlockSpec((N_EXPERTS, tb), lambda p, i: (0, i))],
        out_specs=(pl.BlockSpec((TOP_K, tb), lambda p, i: (0, i * p)),
                   pl.BlockSpec((TOP_K, tb), lambda p, i: (0, i * p)),
                   pl.BlockSpec((N_EXPERTS, LANES), lambda p, i: (0, 0))),
        out_shape=(jax.ShapeDtypeStruct((TOP_K, n), I32), jax.ShapeDtypeStruct((TOP_K, n), F32),
                   jax.ShapeDtypeStruct((N_EXPERTS, LANES), F32)),
        scratch_shapes=[pltpu.VMEM((N_EXPERTS, LANES), F32), pltpu.VMEM((N_EXPERTS, LANES), F32)],
        compiler_params=pltpu.CompilerParams(
            dimension_semantics=("arbitrary", "arbitrary"), vmem_limit_bytes=VMEM_LIMIT),
        name="router",
    )(logits)


def _row_dma_loop(n_tokens, make_copy, wait):
    def body(t0, carry):
        for u in range(ROW_DMA_UNROLL):
            for kk in range(TOP_K):
                copy = make_copy(t0 * ROW_DMA_UNROLL + u, kk)
                if wait:
                    copy.wait()
                else:
                    copy.start(priority=kk % 2)
        return carry

    lax.fori_loop(0, n_tokens // ROW_DMA_UNROLL, body, 0)


def _dispatch_kernel(clear_ref, pos_ref, h2_ref, xs_ref, zero_ref, sem, zsem):
    i = pl.program_id(0)
    tb = pos_ref.shape[1]
    tile_rows = EXPERT_TILE * ROW_TILES
    n_tiles = xs_ref.shape[0] // tile_rows

    @pl.when(i == 0)
    def _():
        zero_ref[...] = jnp.zeros_like(zero_ref)

        def tile_copy(t):
            start_row = pl.multiple_of(t * tile_rows, tile_rows)
            return pltpu.make_async_copy(zero_ref, xs_ref.at[pl.ds(start_row, tile_rows)], zsem)

        def start(t, carry):
            @pl.when(clear_ref[t] > 0)
            def _():
                tile_copy(t).start()
            return carry

        def wait(t, carry):
            @pl.when(clear_ref[t] > 0)
            def _():
                tile_copy(t).wait()
            return carry

        lax.fori_loop(0, n_tiles, start, 0)
        lax.fori_loop(0, n_tiles, wait, 0)

    row_copy = lambda t, kk: pltpu.make_async_copy(_row_tile(h2_ref, t), _row_tile(xs_ref, pos_ref[kk, t]), sem)
    _row_dma_loop(tb, row_copy, wait=False)
    _row_dma_loop(tb, row_copy, wait=True)


def _dispatch(h2_all, pos, clear, n_rows):
    n = h2_all.shape[0] // ROW_TILES
    tb = TOKEN_BLOCK
    return pl.pallas_call(
        _dispatch_kernel,
        grid_spec=pltpu.PrefetchScalarGridSpec(
            num_scalar_prefetch=1,
            grid=(n // tb,),
            in_specs=[pl.BlockSpec((TOP_K, tb), lambda i, clr: (0, i), memory_space=pltpu.SMEM),
                      pl.BlockSpec((tb * ROW_TILES, LANES), lambda i, clr: (i, 0))],
            out_specs=pl.BlockSpec(memory_space=pl.ANY),
            scratch_shapes=[pltpu.VMEM((EXPERT_TILE * ROW_TILES, LANES), F32), pltpu.SemaphoreType.DMA,
                            pltpu.SemaphoreType.DMA],
        ),
        out_shape=jax.ShapeDtypeStruct((n_rows * ROW_TILES, LANES), F32),
        compiler_params=pltpu.CompilerParams(dimension_semantics=("arbitrary",), vmem_limit_bytes=VMEM_LIMIT),
        name="dispatch",
    )(clear, pos, h2_all)


def _expert_kernel(tile_e_ref, nvalid_ref, tile_rows_ref, first_ref, next_e_ref,
                   xs_ref, wg_hbm, bg_ref, wu_hbm, bu_ref, wd_hbm, bd_ref,
                   ys_ref, w_f32, wg_b, wu_b, wd_b, slot_ref, wsem):
    i = pl.program_id(0)
    valid_tile = i < nvalid_ref[0]
    tile_rows = tile_rows_ref[i]
    half = EXPERT_TILE // 2

    def weight_copies(e, s):
        return [pltpu.make_async_copy(w.at[e], w_f32.at[s, m], wsem.at[s, m])
                for m, w in enumerate((wg_hbm, wu_hbm, wd_hbm))]

    @pl.when(i == 0)
    def _():
        slot_ref[0] = 0
        for c in weight_copies(tile_e_ref[0], 0):
            c.start()

    @pl.when(jnp.logical_and(valid_tile, first_ref[i] > 0))
    def _():
        s = slot_ref[0]
        for c in weight_copies(tile_e_ref[i], s):
            c.wait()
        wg_b[...] = w_f32[s, 0].astype(BF16)
        wu_b[...] = w_f32[s, 1].astype(BF16)
        wd_b[...] = w_f32[s, 2].astype(BF16)
        slot_ref[0] = 1 - s

        @pl.when(next_e_ref[i] >= 0)
        def _():
            for c in weight_copies(next_e_ref[i], 1 - s):
                c.start()

    def ffn(rows):
        x = _load_rows(xs_ref, rows).astype(BF16)
        gate = jnp.minimum(_dot(x, wg_b[...]) + bg_ref[0], SWIGLU_LIMIT)
        up = jnp.clip(_dot(x, wu_b[...]) + bu_ref[0], -SWIGLU_LIMIT, SWIGLU_LIMIT)
        glu = gate * _sigmoid(SWIGLU_ALPHA * gate)
        return _dot(((up + 1.0) * glu).astype(BF16), wd_b[...]) + bd_ref[0]

    @pl.when(jnp.logical_and(valid_tile, tile_rows > half))
    def _():
        _store_rows(ys_ref, ffn(EXPERT_TILE))

    @pl.when(jnp.logical_and(valid_tile, tile_rows <= half))
    def _():
        _store_rows(ys_ref.at[pl.ds(0, half * ROW_TILES)], ffn(half))
        ys_ref[half * ROW_TILES:, :] = jnp.zeros((half * ROW_TILES, LANES), F32)

    @pl.when(i >= nvalid_ref[0])
    def _():
        ys_ref[...] = jnp.zeros_like(ys_ref)


def _experts(xs, tile_expert, n_valid, tile_rows, first, next_expert, w_gate, b_gate, w_up, b_up, w_down, b_down):
    d, d_ff = w_gate.shape[1], w_gate.shape[2]
    assert d == d_ff, "one [2, 3, d, d_ff] scratch holds all three f32 weight matrices"
    tm = EXPERT_TILE
    n_tiles = xs.shape[0] // (tm * ROW_TILES)
    row_map = lambda i, te, nv, *_: (jnp.minimum(i, nv[0] - 1), 0)
    b_map = lambda i, te, *_: (te[i], 0, 0)
    hbm = pl.BlockSpec(memory_space=pl.ANY)
    return pl.pallas_call(
        _expert_kernel,
        grid_spec=pltpu.PrefetchScalarGridSpec(
            num_scalar_prefetch=5,
            grid=(n_tiles,),
            in_specs=[pl.BlockSpec((tm * ROW_TILES, LANES), row_map),
                      hbm, pl.BlockSpec((1, 1, d_ff), b_map),
                      hbm, pl.BlockSpec((1, 1, d_ff), b_map),
                      hbm, pl.BlockSpec((1, 1, d), b_map)],
            out_specs=pl.BlockSpec((tm * ROW_TILES, LANES), lambda i, *_: (i, 0)),
            scratch_shapes=[pltpu.VMEM((2, 3, d, d_ff), F32),
                            pltpu.VMEM((d, d_ff), BF16), pltpu.VMEM((d, d_ff), BF16), pltpu.VMEM((d_ff, d), BF16),
                            pltpu.SMEM((1,), I32), pltpu.SemaphoreType.DMA((2, 3))],
        ),
        out_shape=jax.ShapeDtypeStruct(xs.shape, F32),
        compiler_params=pltpu.CompilerParams(dimension_semantics=("arbitrary",), vmem_limit_bytes=VMEM_LIMIT),
        name="experts",
    )(tile_expert, n_valid, tile_rows, first, next_expert, xs, w_gate, b_gate[:, None, :], w_up, b_up[:, None, :],
      w_down, b_down[:, None, :])


def _combine_kernel(pos_ref, pos_next_ref, x1_ref, wt_ref, g_final_ref, ys_ref, yp_ref, ysmp_ref, buf_ref, sem):
    i = pl.program_id(0)
    last = pl.num_programs(0) - 1
    tb = x1_ref.shape[0]
    slot = i % 2

    def row_copy(block_pos_ref, s):
        return lambda t, kk: pltpu.make_async_copy(
            _row_tile(ys_ref, block_pos_ref[kk, t]), _row_tile(buf_ref.at[s, kk], t), sem.at[s])

    @pl.when(i == 0)
    def _():
        _row_dma_loop(tb, row_copy(pos_ref, 0), wait=False)

    @pl.when(i < last)
    def _():
        _row_dma_loop(tb, row_copy(pos_next_ref, 1 - slot), wait=False)

    _row_dma_loop(tb, row_copy(pos_ref, slot), wait=True)
    out = x1_ref[...]
    for kk in range(TOP_K):
        out = out + wt_ref[:, kk:kk + 1] * _load_rows(buf_ref.at[slot, kk], tb)
    y = _rms(out, g_final_ref[...])

    @pl.when(i < last)
    def _():
        yp_ref[0] = y

    @pl.when(i == last)
    def _():
        ysmp_ref[...] = y[0:ysmp_ref.shape[0], :]


def _combine(x1_all, pos, wt_cols, ys, g_final_row, batch, seq, ns):
    d = x1_all.shape[1]
    tb = TOKEN_BLOCK
    nt = seq // tb
    n_prompt_blocks = batch * nt
    nb = n_prompt_blocks + 1

    def prompt_block_index(i):
        i = jnp.minimum(i, n_prompt_blocks - 1)
        step, s = i // SEQS_PER_STEP, i % SEQS_PER_STEP
        return ((step // nt) * SEQS_PER_STEP + s, step % nt, 0)

    return pl.pallas_call(
        _combine_kernel,
        grid=(nb,),
        in_specs=[pl.BlockSpec((TOP_K, tb), lambda i: (0, i), memory_space=pltpu.SMEM),
                  pl.BlockSpec((TOP_K, tb), lambda i: (0, jnp.minimum(i + 1, nb - 1)), memory_space=pltpu.SMEM),
                  pl.BlockSpec((tb, d), lambda i: (i, 0)),
                  pl.BlockSpec((tb, TOP_K), lambda i: (i, 0)),
                  pl.BlockSpec((1, d), lambda i: (0, 0)),
                  pl.BlockSpec(memory_space=pl.ANY)],
        out_specs=(pl.BlockSpec((1, tb, d), prompt_block_index),
                   pl.BlockSpec((ns, d), lambda i: (0, 0))),
        out_shape=(jax.ShapeDtypeStruct((batch, seq, d), F32), jax.ShapeDtypeStruct((ns, d), F32)),
        scratch_shapes=[pltpu.VMEM((2, TOP_K, tb * ROW_TILES, LANES), F32), pltpu.SemaphoreType.DMA((2,))],
        compiler_params=pltpu.CompilerParams(dimension_semantics=("arbitrary",), vmem_limit_bytes=VMEM_LIMIT),
        name="combine",
    )(pos, pos, x1_all, wt_cols, g_final_row, ys)


def _prepare_weights(g_mix, w_in, conv_w, conv_b, b_igate, b_fgate, g_mhnorm, w_a, w_spatial, b_spatial, g_vnorm,
                     w_b, w_out, g_ffn, w_router, b_router):
    d = w_in.shape[0]
    gate_lo = 2 * D_QK + D_VAL
    gate_hi = gate_lo + GATE_COLS
    row = lambda a: a.reshape(1, -1).astype(F32)
    w_if = w_in[:, gate_lo:gate_hi]
    gate_b = jnp.concatenate([b_igate, b_fgate])
    tril = jnp.tril(jnp.ones((SPATIAL_CHUNK, SPATIAL_CHUNK), w_spatial.dtype))
    chunks = PROMPT_BLOCK // SPATIAL_CHUNK
    head_of_lane = jnp.arange(D_QK) // D_K
    seg = (head_of_lane[:, None] == jnp.arange(LANES)[None, :]).astype(F32)
    pad_cols = lambda a: jnp.pad(a, ((0, 0), (0, LANES - N_HEADS)))
    return dict(
        g_mix=row(g_mix),
        w_lo=w_in[:, :gate_lo].astype(BF16), w_hi=w_in[:, gate_hi:].astype(BF16),
        w_ig=pad_cols(w_if[:, :N_HEADS]).astype(BF16),
        w_fg=pad_cols(w_if[:, N_HEADS:]).astype(BF16),
        w_ift=w_if.T.astype(BF16),
        w_gates=jnp.pad(w_if, ((0, 0), (0, LANES - GATE_COLS))).astype(BF16),
        b_gates=jnp.pad(gate_b, (0, LANES - GATE_COLS)).reshape(1, LANES),
        b_ig=pad_cols(b_igate.reshape(1, N_HEADS)),
        b_fg=pad_cols(b_fgate.reshape(1, N_HEADS)),
        gate_b_col=jnp.broadcast_to(gate_b[:, None], (GATE_COLS, LANES)),
        conv_w=conv_w, conv_b=row(conv_b), g_mh=row(g_mhnorm), w_a=w_a.astype(BF16),
        ws_masked=jnp.stack([jnp.kron(jnp.eye(chunks, dtype=w_spatial.dtype), w_g)
                             for w_g in w_spatial * tril]).astype(BF16),
        bsp_full=jnp.tile(jnp.repeat(b_spatial.T, GROUP_DIM, axis=1), (chunks, 1)),
        wsp_row=jnp.repeat(w_spatial[:, 0, 0], GROUP_DIM).reshape(1, -1),
        bsp_row=jnp.repeat(b_spatial[:, 0], GROUP_DIM).reshape(1, -1),
        g_vn=row(g_vnorm), w_b=w_b.astype(BF16), w_out=w_out.astype(BF16), g_ffn=row(g_ffn),
        w_rt=w_router.T, b_r_col=jnp.broadcast_to(b_router[:, None], (N_EXPERTS, LANES)),
        seg=seg, segt=seg.T,
    )


def _tile_metadata(counts, n_tiles):
    tiles_per_expert = (counts + EXPERT_TILE - 1) // EXPERT_TILE
    tile_end = jnp.cumsum(tiles_per_expert)
    n_valid = tile_end[-1]
    tile_ids = jnp.minimum(jnp.arange(n_tiles, dtype=I32), n_valid - 1)
    tile_expert = jnp.sum(tile_end[None, :] <= tile_ids[:, None], axis=1).astype(I32)
    tail_tile = jnp.where(counts > 0, tile_end - 1, -1)
    all_tiles = jnp.arange(n_tiles, dtype=I32)
    clear = jnp.logical_or(all_tiles >= n_valid, jnp.any(all_tiles[:, None] == tail_tile[None, :], axis=1))
    expert_of = tile_expert[:, None] == jnp.arange(counts.shape[0], dtype=I32)[None, :]
    pick = lambda per_expert: jnp.sum(jnp.where(expert_of, per_expert[None, :], 0), axis=1)
    tile_in_expert = tile_ids - pick(tile_end - tiles_per_expert)
    tile_rows = jnp.clip(pick(counts) - tile_in_expert * EXPERT_TILE, 0, EXPERT_TILE).astype(I32)
    first = jnp.logical_and(all_tiles < n_valid, tile_in_expert == 0).astype(I32)
    following_tile = pick(tile_end)
    next_expert = jnp.sum(jnp.where(all_tiles[None, :] == following_tile[:, None], tile_expert[None, :], 0), axis=1)
    next_expert = jnp.where(following_tile < n_valid, next_expert, -1).astype(I32)
    return tile_expert, n_valid.reshape(1).astype(I32), clear.astype(I32), tile_rows, first, next_expert


def kernel(x_prompt, x_sample, state_conv, state_C, state_n, state_m, g_mix, w_in, conv_w, conv_b, b_igate, b_fgate,
           g_mhnorm, w_a, w_spatial, b_spatial, g_vnorm, w_b, w_out, g_ffn, w_router, b_router, w_gate, b_gate,
           w_up, b_up, w_down, b_down, g_final):
    depth = g_mix.shape[0]
    assert depth == 1, "single-layer trunk"
    batch, seq, d = x_prompt.shape
    ns = x_sample.shape[0]
    assert seq % PROMPT_BLOCK == 0 and ns <= TOKEN_BLOCK and PROMPT_BLOCK == TOKEN_BLOCK
    assert batch % SEQS_PER_STEP == 0
    assert PROMPT_BLOCK % MLSTM_CHUNK == 0 and MLSTM_CHUNK % SPATIAL_CHUNK == 0
    wts = _prepare_weights(g_mix[0], w_in[0], conv_w[0], conv_b[0], b_igate[0], b_fgate[0], g_mhnorm[0], w_a[0],
                           w_spatial[0], b_spatial[0], g_vnorm[0], w_b[0], w_out[0], g_ffn[0], w_router[0],
                           b_router[0])

    x1_s, h2_s, logits_s, s_conv, s_c, s_n, s_m, vg = _sample_mixer(
        x_sample, state_conv[0], state_C[0], state_n[0], state_m[0], wts)
    x1_all, h2_all, logits, p_conv, p_c, p_n, p_m = _prompt_mixer(x_prompt, x1_s, h2_s, logits_s, wts)
    n_total = x1_all.shape[0]

    pos, wt, counts = _router(logits)
    n_tiles = -(-(n_total * TOP_K) // EXPERT_TILE) + N_EXPERTS
    tile_expert, n_valid, clear, tile_rows, first, next_expert = _tile_metadata(counts[:, 0].astype(I32), n_tiles)
    xs = _dispatch(h2_all, pos, clear, n_tiles * EXPERT_TILE)
    ys = _experts(xs, tile_expert, n_valid, tile_rows, first, next_expert, w_gate[0], b_gate[0], w_up[0], b_up[0],
                  w_down[0], b_down[0])
    y_prompt, y_sample = _combine(x1_all, pos, wt.T, ys, g_final.reshape(1, d), batch, seq, ns)

    return (y_prompt, y_sample.reshape(ns, 1, d),
            p_conv[None], p_c[None], p_n[None], p_m[None, :, :, 0],
            s_conv[None], s_c[None], s_n[None], s_m[None], vg.reshape(1, ns, 1, N_GROUPS, GROUP_DIM))
```

```python
import functools

import jax
import jax.numpy as jnp
from jax import lax
from jax.experimental import pallas as pl
from jax.experimental.pallas import tpu as pltpu

F32 = jnp.float32
BF16 = jnp.bfloat16
I32 = jnp.int32
HIGHEST = lax.Precision.HIGHEST

N_HEADS = 8
D_K = 64
D_V = 128
D_QK = N_HEADS * D_K
D_VAL = N_HEADS * D_V
CONV_TAPS = 4
GATE_COLS = 2 * N_HEADS
N_GROUPS = 8
GROUP_DIM = 128
SPATIAL_CHUNK = 128
N_EXPERTS = 32
TOP_K = 4
SWIGLU_LIMIT = 7.0
SWIGLU_ALPHA = 1.702
RMS_EPS = 1e-6
EMPTY_MAX = -1e30
LANES = 128
SUBLANES = 8
D_MODEL = 1024
ROW_TILES = D_MODEL // LANES

PROMPT_BLOCK = 256
MLSTM_CHUNK = 256
SEQS_PER_STEP = 1
SAMPLE_STATE_BLOCK = 8
TOKEN_BLOCK = 256
ROW_DMA_UNROLL = 2
ROUTER_BLOCK_MAX = 1280
DISPATCH_BLOCK_MAX = 1280
EXPERT_TILE = 512
VMEM_LIMIT = 56 * 1024 * 1024


def _dot(a, b, precision=None):
    return jnp.dot(a, b, preferred_element_type=F32, precision=precision)


def _dot_nt(a, b, precision=None):
    return lax.dot_general(a, b, (((1,), (1,)), ((), ())), preferred_element_type=F32, precision=precision)


def _dot_tn(a, b, precision=None):
    return lax.dot_general(a, b, (((0,), (0,)), ((), ())), preferred_element_type=F32, precision=precision)


def _rms(x, g):
    return x * lax.rsqrt(jnp.mean(x * x, axis=-1, keepdims=True) + RMS_EPS) * g


def _sigmoid(x):
    return jax.nn.sigmoid(x)


def _gelu(x):
    return 0.5 * x * (1.0 + lax.erf(x * (2.0 ** -0.5)))


def _log_sigmoid(x):
    return jax.nn.log_sigmoid(x)


def _store_rows(ref, val):
    rows = val.shape[0]
    for jt in range(ROW_TILES):
        ref[pl.ds(jt, rows, stride=ROW_TILES), :] = val[:, jt * LANES:(jt + 1) * LANES]


def _load_rows(ref, rows):
    return jnp.concatenate([ref[pl.ds(jt, rows, stride=ROW_TILES), :] for jt in range(ROW_TILES)], axis=1)


def _row_tile(ref, r):
    return ref.at[pl.ds(pl.multiple_of(r * ROW_TILES, ROW_TILES), ROW_TILES)]


def _in_projection(hn, w_main_ref, piece):
    ref, idx = (w_main_ref[0], piece) if piece < 2 else (w_main_ref[1], piece - 2)
    return _dot(hn, ref[:, idx * 1024:(idx + 1) * 1024])


def _post_mixer(x, hm, z_o, z_u, z_vg, z_ga, z_gb, spatial_fn,
                g_mh_ref, w_a_ref, g_vn_ref, w_b_ref, w_out_ref):
    parts = []
    for h in range(N_HEADS):
        hh = hm[:, h * D_V:(h + 1) * D_V]
        parts.append(hh * lax.rsqrt(jnp.mean(hh * hh, axis=-1, keepdims=True) + RMS_EPS))
    hmn = jnp.concatenate(parts, axis=1) * g_mh_ref[...]
    y_a = _dot((hmn * _sigmoid(z_o)).astype(BF16), w_a_ref[...])
    u = _gelu(z_u)
    vg = _rms(_gelu(z_vg), g_vn_ref[...])
    s = spatial_fn(vg)
    y_b = _dot((u * s).astype(BF16), w_b_ref[...])
    merged = _sigmoid(z_ga) * y_a + _sigmoid(z_gb) * y_b
    x1 = x + _dot(merged.astype(BF16), w_out_ref[...])
    return x1, vg


def _ffn_input(x1, g_ffn_ref, w_rt_ref, b_r_ref):
    h2 = _rms(x1, g_ffn_ref[...])
    return h2, _dot_nt(w_rt_ref[...], h2, HIGHEST) + b_r_ref[:, 0:1]


def _prompt_mixer_kernel(x_ref, x1s_ref, h2s_ref, logits_s_ref, *refs, blocks_per_seq, n_prompt_steps):
    g = pl.program_id(0)
    block_weights = refs[:16]
    g_ffn_ref, w_rt_ref, b_r_ref = refs[16:19]
    x1_ref, h2_ref, logits_ref, pconv_ref, pc_ref, pn_ref, pm_ref, xp_ref, x1_keep_ref = refs[19:]
    tb = x_ref.shape[1]

    def finish_previous_block():
        for s in range(SEQS_PER_STEP):
            h2, logits = _ffn_input(x1_keep_ref[s], g_ffn_ref, w_rt_ref, b_r_ref)
            _store_rows(h2_ref.at[pl.ds(s * tb * ROW_TILES, tb * ROW_TILES)], h2)
            logits_ref[:, s * tb:(s + 1) * tb] = logits

    @pl.when(g == 0)
    def _():
        x1_keep_ref[...] = jnp.zeros_like(x1_keep_ref)

    @pl.when(jnp.logical_and(g < n_prompt_steps, g % blocks_per_seq == 0))
    def _():
        pc_ref[...] = jnp.zeros_like(pc_ref)
        pn_ref[...] = jnp.zeros_like(pn_ref)
        pm_ref[...] = jnp.full(pm_ref.shape, EMPTY_MAX, F32)
        xp_ref[:, 0:SUBLANES, :] = jnp.zeros((xp_ref.shape[0], SUBLANES, xp_ref.shape[2]), F32)

    @pl.when(g < n_prompt_steps)
    def _():
        finish_previous_block()
        for s in range(SEQS_PER_STEP):
            _prompt_block(x_ref.at[s], *block_weights, x1_ref.at[pl.ds(s * tb, tb)],
                          pconv_ref.at[s], pc_ref.at[s], pn_ref.at[s], pm_ref.at[s], xp_ref.at[s], x1_keep_ref.at[s])

    ns = x1s_ref.shape[0]
    n_pad = x1_ref.shape[0] - ns

    @pl.when(g == n_prompt_steps)
    def _():
        finish_previous_block()
        x1_ref[0:ns, :] = x1s_ref[...]
        x1_ref[ns:, :] = jnp.zeros((n_pad, x1_ref.shape[1]), F32)

    @pl.when(g == n_prompt_steps + 1)
    def _():
        h2_ref[0:ns * ROW_TILES, :] = h2s_ref[...]
        h2_ref[ns * ROW_TILES:, :] = jnp.zeros((n_pad * ROW_TILES, LANES), F32)
        logits_ref[:, 0:ns] = logits_s_ref[...]
        logits_ref[:, ns:] = jnp.broadcast_to(b_r_ref[:, 0:1], (N_EXPERTS, n_pad))


def _prompt_block(x_ref, g_mix_ref, w_lo_ref, w_hi_ref, w_gates_ref, w_ift_ref, b_gates_ref,
                  gate_b_col_ref, conv_w_ref, conv_b_ref, g_mh_ref, w_a_ref, ws_ref, bsp_ref, g_vn_ref,
                  w_b_ref, w_out_ref,
                  x1_ref, pconv_ref, pc_ref, pn_ref, pm_ref, xp_ref, x1_keep_ref):
    w_main_ref = (w_lo_ref, w_hi_ref)
    tb = x_ref.shape[0]
    x = x_ref[...]
    hn = _rms(x, g_mix_ref[...]).astype(BF16)

    z_qk = _in_projection(hn, w_main_ref, 0)
    xp_ref[SUBLANES:SUBLANES + tb, :] = z_qk
    y = conv_b_ref[...] + conv_w_ref[CONV_TAPS - 1:CONV_TAPS, :] * z_qk
    for tap in range(CONV_TAPS - 1):
        back = CONV_TAPS - 1 - tap
        y = y + conv_w_ref[tap:tap + 1, :] * xp_ref[SUBLANES - back:SUBLANES - back + tb, :]
    tail = z_qk[tb - (CONV_TAPS - 1):tb, :]
    xp_ref[SUBLANES - (CONV_TAPS - 1):SUBLANES, :] = tail
    pconv_ref[...] = tail
    qk = y * _sigmoid(y)
    q = qk[:, :D_QK]
    k = qk[:, D_QK:] * (D_K ** -0.5)
    qb = q.astype(BF16)
    v = _in_projection(hn, w_main_ref, 1)
    vb = v.astype(BF16)

    ig_c = _dot(hn, w_gates_ref[...]) + b_gates_ref[...]
    lf_c = _log_sigmoid(ig_c)
    gr = _dot_nt(w_ift_ref[...], hn) + gate_b_col_ref[:, 0:1]
    ig_r = gr[0:N_HEADS, :]
    lf_r = _log_sigmoid(gr[N_HEADS:GATE_COLS, :])
    ch = MLSTM_CHUNK
    row_i = lax.broadcasted_iota(I32, (ch, ch), 0)
    col_i = lax.broadcasted_iota(I32, (ch, ch), 1)
    causal = col_i <= row_i
    tri = causal.astype(F32)

    hm_chunks = []
    for c0 in range(0, tb, ch):
        rows = slice(c0, c0 + ch)
        b_c = _dot(tri, lf_c[rows, :], HIGHEST)
        b_r = _dot_nt(lf_r[:, rows], tri, HIGHEST)
        hm_parts = []
        for h in range(N_HEADS):
            bt = b_c[:, N_HEADS + h:N_HEADS + h + 1]
            ig_col = ig_c[rows, h:h + 1]
            bs = b_r[h:h + 1, :]
            ig_row = ig_r[h:h + 1, rows]
            m_prev = pm_ref[h:h + 1, 0:1]
            c_prev = pc_ref[h]
            n_prev = pn_ref[h:h + 1, :]
            q_h = q[rows, h * D_K:(h + 1) * D_K]
            qb_h = qb[rows, h * D_K:(h + 1) * D_K]
            k_h = k[rows, h * D_K:(h + 1) * D_K]
            vb_h = vb[rows, h * D_V:(h + 1) * D_V]

            dmat = jnp.where(causal, bt - bs + ig_row, -jnp.inf)
            inter = bt + m_prev
            m_t = jnp.maximum(inter, jnp.max(dmat, axis=-1, keepdims=True))
            w_inter = jnp.exp(inter - m_t)
            s = _dot_nt(qb_h, k_h.astype(BF16)) * jnp.exp(dmat - m_t)
            num = w_inter * _dot(qb_h, c_prev.astype(BF16)) + _dot(s.astype(BF16), vb_h)
            den = w_inter * jnp.sum(q_h * n_prev, axis=-1, keepdims=True) + jnp.sum(s, axis=-1, keepdims=True)
            hm_parts.append(num / jnp.maximum(jnp.abs(den), jnp.exp(-m_t)))

            m_new = m_t[ch - 1:ch, :]
            b_last = bt[ch - 1:ch, :]
            decay = jnp.exp(b_last + m_prev - m_new)
            w_s = jnp.exp(b_last - bt + ig_col - m_new)
            kw = k_h * w_s
            pc_ref[h] = decay * c_prev + _dot_tn(kw.astype(BF16), vb_h)
            pn_ref[h:h + 1, :] = decay * n_prev + jnp.sum(kw, axis=0, keepdims=True)
            pm_ref[h:h + 1, :] = jnp.broadcast_to(m_new, (1, pm_ref.shape[1]))
        hm_chunks.append(jnp.concatenate(hm_parts, axis=1))
    hm = jnp.concatenate(hm_chunks, axis=0)

    def spatial(vg):
        vgb = vg.astype(BF16)
        blocks = [_dot(ws_ref[g], vgb[:, g * GROUP_DIM:(g + 1) * GROUP_DIM]) for g in range(N_GROUPS)]
        return jnp.concatenate(blocks, axis=1) + bsp_ref[...]

    z_o, z_u, z_vg, z_ga, z_gb = [_in_projection(hn, w_main_ref, 2 + p) for p in range(5)]
    x1, _ = _post_mixer(x, hm, z_o, z_u, z_vg, z_ga, z_gb, spatial, g_mh_ref, w_a_ref, g_vn_ref, w_b_ref, w_out_ref)
    x1_ref[...] = x1
    x1_keep_ref[...] = x1


def _whole_vmem():
    return pl.BlockSpec(memory_space=pltpu.VMEM)


def _prompt_mixer(x_prompt, x1_sample, h2_sample, logits_sample, wts):
    batch, seq, d = x_prompt.shape
    tb = PROMPT_BLOCK
    sps = SEQS_PER_STEP
    nt = seq // tb
    n_steps = (batch // sps) * nt
    step_rows = sps * tb
    n_total = (n_steps + 1) * step_rows
    out_shapes = (
        jax.ShapeDtypeStruct((n_total, d), F32),
        jax.ShapeDtypeStruct((n_total * ROW_TILES, LANES), F32),
        jax.ShapeDtypeStruct((N_EXPERTS, n_total), F32),
        jax.ShapeDtypeStruct((batch, CONV_TAPS - 1, 2 * D_QK), F32),
        jax.ShapeDtypeStruct((batch, N_HEADS, D_K, D_V), F32),
        jax.ShapeDtypeStruct((batch, N_HEADS, D_K), F32),
        jax.ShapeDtypeStruct((batch, N_HEADS, LANES), F32),
    )
    group_of = lambda g: jnp.minimum(g, n_steps - 1) // nt
    lagged = lambda g: jnp.clip(g - 1, 0, n_steps)
    out_specs = (
        pl.BlockSpec((step_rows, d), lambda g: (jnp.minimum(g, n_steps), 0)),
        pl.BlockSpec((step_rows * ROW_TILES, LANES), lambda g: (lagged(g), 0)),
        pl.BlockSpec((N_EXPERTS, step_rows), lambda g: (0, lagged(g))),
        pl.BlockSpec((sps, CONV_TAPS - 1, 2 * D_QK), lambda g: (group_of(g), 0, 0)),
        pl.BlockSpec((sps, N_HEADS, D_K, D_V), lambda g: (group_of(g), 0, 0, 0)),
        pl.BlockSpec((sps, N_HEADS, D_K), lambda g: (group_of(g), 0, 0)),
        pl.BlockSpec((sps, N_HEADS, LANES), lambda g: (group_of(g), 0, 0)),
    )
    in_specs = ([pl.BlockSpec((sps, tb, d), lambda g: (group_of(g), jnp.minimum(g, n_steps - 1) % nt, 0))]
                + [_whole_vmem()] * 22)
    return pl.pallas_call(
        functools.partial(_prompt_mixer_kernel, blocks_per_seq=nt, n_prompt_steps=n_steps),
        grid=(n_steps + 2,),
        in_specs=in_specs,
        out_specs=out_specs,
        out_shape=out_shapes,
        scratch_shapes=[pltpu.VMEM((sps, SUBLANES + tb, 2 * D_QK), F32), pltpu.VMEM((sps, tb, d), F32)],
        compiler_params=pltpu.CompilerParams(dimension_semantics=("arbitrary",), vmem_limit_bytes=VMEM_LIMIT),
        name="prompt_mixer",
    )(x_prompt, x1_sample, h2_sample, logits_sample,
      wts["g_mix"], wts["w_lo"], wts["w_hi"], wts["w_gates"], wts["w_ift"], wts["b_gates"],
      wts["gate_b_col"], wts["conv_w"], wts["conv_b"], wts["g_mh"], wts["w_a"], wts["ws_masked"], wts["bsp_full"],
      wts["g_vn"], wts["w_b"], wts["w_out"], wts["g_ffn"], wts["w_rt"], wts["b_r_col"])


def _sample_pre_kernel(x_ref, c0_ref, c1_ref, c2_ref, n_ref, m_ref,
                       g_mix_ref, w_lo_ref, w_hi_ref, w_ig_ref, w_fg_ref, b_ig_ref, b_fg_ref, conv_w_ref, conv_b_ref,
                       seg_ref, segt_ref,
                       qw_ref, kw_ref, v_ref, scal_ref, nnew_ref, zqk_ref, rest_ref):
    w_main_ref = (w_lo_ref, w_hi_ref)
    x = x_ref[...]
    hn = _rms(x, g_mix_ref[...]).astype(BF16)
    z_qk = _in_projection(hn, w_main_ref, 0)
    zqk_ref[...] = z_qk
    y = (conv_b_ref[...] + conv_w_ref[0:1, :] * c0_ref[...] + conv_w_ref[1:2, :] * c1_ref[...]
         + conv_w_ref[2:3, :] * c2_ref[...] + conv_w_ref[3:4, :] * z_qk)
    qk = y * _sigmoid(y)
    q = qk[:, :D_QK]
    k = qk[:, D_QK:] * (D_K ** -0.5)
    v_ref[...] = _in_projection(hn, w_main_ref, 1)
    for p in range(5):
        rest_ref[:, p * 1024:(p + 1) * 1024] = _in_projection(hn, w_main_ref, 2 + p)

    ig = _dot(hn, w_ig_ref[...]) + b_ig_ref[...]
    lf = _log_sigmoid(_dot(hn, w_fg_ref[...]) + b_fg_ref[...])
    m_prev = m_ref[...]
    m_new = jnp.maximum(lf + m_prev, ig)
    decay = jnp.exp(lf + m_prev - m_new)
    w_s = jnp.exp(ig - m_new)
    n_prev = n_ref[...]
    seg = seg_ref[...]
    segt = segt_ref[...]
    qk_dot = _dot(q * k, seg, HIGHEST)
    qn_dot = _dot(q * n_prev, seg, HIGHEST)
    s = qk_dot * w_s
    den = decay * qn_dot + s
    denom = jnp.maximum(jnp.abs(den), jnp.exp(-m_new))
    decay_x = _dot(decay, segt, HIGHEST)
    ws_x = _dot(w_s, segt, HIGHEST)
    qw_ref[...] = q * decay_x
    kw = k * ws_x
    kw_ref[...] = kw
    nnew_ref[...] = decay_x * n_prev + kw
    scal_ref[0] = decay
    scal_ref[1] = s
    scal_ref[2] = denom
    scal_ref[3] = m_new


def _sample_state_kernel(c_ref, qw_ref, kw_ref, v_ref, scal_ref, eye_ref, cnew_ref, h_ref):
    bb = qw_ref.shape[0]
    eye = eye_ref[...]
    q_t = _dot_tn(qw_ref[...], eye, HIGHEST)
    k_t = _dot_tn(kw_ref[...], eye, HIGHEST)
    decay = _dot_tn(scal_ref[0], eye, HIGHEST)
    s_qk = _dot_tn(scal_ref[1], eye, HIGHEST)
    denom = _dot_tn(scal_ref[2], eye, HIGHEST)
    for b in range(bb):
        for h in range(N_HEADS):
            rows = slice(h * D_K, (h + 1) * D_K)
            c = c_ref[b, h]
            v_row = v_ref[b:b + 1, h * D_V:(h + 1) * D_V]
            cnew_ref[b, h] = decay[h:h + 1, b:b + 1] * c + k_t[rows, b:b + 1] * v_row
            num = jnp.sum(q_t[rows, b:b + 1] * c, axis=0, keepdims=True) + s_qk[h:h + 1, b:b + 1] * v_row
            h_ref[b:b + 1, h * D_V:(h + 1) * D_V] = num / denom[h:h + 1, b:b + 1]


def _sample_post_kernel(x_ref, hm_ref, rest_ref,
                        wsp_ref, bsp_ref, g_mh_ref, w_a_ref, g_vn_ref, w_b_ref, w_out_ref, g_ffn_ref, w_rt_ref,
                        b_r_ref, x1_ref, h2_ref, vg_ref, logits_ref):
    z = [rest_ref[:, p * 1024:(p + 1) * 1024] for p in range(5)]
    spatial = lambda vg: vg * wsp_ref[...] + bsp_ref[...]
    x1, vg = _post_mixer(x_ref[...], hm_ref[...], z[0], z[1], z[2], z[3], z[4], spatial,
                         g_mh_ref, w_a_ref, g_vn_ref, w_b_ref, w_out_ref)
    h2, logits = _ffn_input(x1, g_ffn_ref, w_rt_ref, b_r_ref)
    x1_ref[...] = x1
    _store_rows(h2_ref, h2)
    vg_ref[...] = vg
    logits_ref[...] = logits


def _sample_mixer(x_sample, state_conv, state_c, state_n, state_m, wts):
    ns, _, d = x_sample.shape
    xs = x_sample.reshape(ns, d)
    c_taps = [state_conv[:, t, :] for t in range(CONV_TAPS - 1)]
    n_rows = state_n.reshape(ns, D_QK)
    pre_out = (
        jax.ShapeDtypeStruct((ns, D_QK), F32),
        jax.ShapeDtypeStruct((ns, D_QK), F32),
        jax.ShapeDtypeStruct((ns, D_VAL), F32),
        jax.ShapeDtypeStruct((4, ns, LANES), F32),
        jax.ShapeDtypeStruct((ns, D_QK), F32),
        jax.ShapeDtypeStruct((ns, 2 * D_QK), F32),
        jax.ShapeDtypeStruct((ns, 5 * 1024), F32),
    )
    qw, kw, v, scal, n_new, z_qk, rest = pl.pallas_call(
        _sample_pre_kernel,
        out_shape=pre_out,
        in_specs=[_whole_vmem()] * 17,
        out_specs=tuple(_whole_vmem() for _ in pre_out),
        compiler_params=pltpu.CompilerParams(vmem_limit_bytes=VMEM_LIMIT),
        name="sample_pre",
    )(xs, c_taps[0], c_taps[1], c_taps[2], n_rows, jnp.pad(state_m, ((0, 0), (0, LANES - N_HEADS))),
      wts["g_mix"], wts["w_lo"], wts["w_hi"], wts["w_ig"], wts["w_fg"], wts["b_ig"], wts["b_fg"], wts["conv_w"],
      wts["conv_b"], wts["seg"], wts["segt"])
    bb = SAMPLE_STATE_BLOCK
    eye = (jnp.arange(bb)[:, None] == jnp.arange(LANES)[None, :]).astype(F32)
    rows2 = lambda width: pl.BlockSpec((bb, width), lambda i: (i, 0))
    c_blk = pl.BlockSpec((bb, N_HEADS, D_K, D_V), lambda i: (i, 0, 0, 0))
    c_new, hm = pl.pallas_call(
        _sample_state_kernel,
        grid=(ns // bb,),
        in_specs=[c_blk, rows2(D_QK), rows2(D_QK), rows2(D_VAL),
                  pl.BlockSpec((4, bb, LANES), lambda i: (0, i, 0)),
                  pl.BlockSpec((bb, LANES), lambda i: (0, 0))],
        out_specs=(c_blk, rows2(D_VAL)),
        out_shape=(jax.ShapeDtypeStruct((ns, N_HEADS, D_K, D_V), F32), jax.ShapeDtypeStruct((ns, D_VAL), F32)),
        compiler_params=pltpu.CompilerParams(dimension_semantics=("arbitrary",), vmem_limit_bytes=VMEM_LIMIT),
        name="sample_state",
    )(state_c, qw, kw, v, scal, eye)
    scal = scal[:, :, :N_HEADS]

    post_out = (jax.ShapeDtypeStruct((ns, d), F32), jax.ShapeDtypeStruct((ns * ROW_TILES, LANES), F32),
                jax.ShapeDtypeStruct((ns, d), F32), jax.ShapeDtypeStruct((N_EXPERTS, ns), F32))
    x1_s, h2_s, vg, logits_s = pl.pallas_call(
        _sample_post_kernel,
        out_shape=post_out,
        in_specs=[_whole_vmem()] * 13,
        out_specs=tuple(_whole_vmem() for _ in post_out),
        compiler_params=pltpu.CompilerParams(vmem_limit_bytes=VMEM_LIMIT),
        name="sample_post",
    )(xs, hm, rest, wts["wsp_row"], wts["bsp_row"], wts["g_mh"], wts["w_a"], wts["g_vn"],
      wts["w_b"], wts["w_out"], wts["g_ffn"], wts["w_rt"], wts["b_r_col"])
    s_conv = jnp.concatenate([state_conv[:, 1:, :], z_qk[:, None, :]], axis=1)
    return x1_s, h2_s, logits_s, s_conv, c_new, n_new.reshape(ns, N_HEADS, D_K), scal[3], vg


def _router_kernel(logits_ref, pos_ref, wt_ref, cnt_ref, run_ref, off_ref):
    phase = pl.program_id(0)
    i = pl.program_id(1)
    tb = logits_ref.shape[1]

    @pl.when(jnp.logical_and(phase == 0, i == 0))
    def _():
        cnt_ref[...] = jnp.zeros_like(cnt_ref)

    logits = logits_ref[...]
    e_iota = lax.broadcasted_iota(I32, logits.shape, 0)
    sels, vals = [], []
    work = logits
    for _ in range(TOP_K):
        mx = jnp.max(work, axis=0, keepdims=True)
        idx = jnp.min(jnp.where(work == mx, e_iota, N_EXPERTS), axis=0, keepdims=True)
        sel = e_iota == idx
        work = jnp.where(sel, -jnp.inf, work)
        sels.append(sel.astype(F32))
        vals.append(mx)
    sel_all = sels[0] + sels[1] + sels[2] + sels[3]
    block_count = jnp.sum(sel_all, axis=1, keepdims=True)

    @pl.when(phase == 0)
    def _():
        cnt_ref[...] += jnp.broadcast_to(block_count, cnt_ref.shape)

    @pl.when(jnp.logical_and(phase == 1, i == 0))
    def _():
        cnt = cnt_ref[...]
        padded = jnp.ceil(cnt * (1.0 / EXPERT_TILE)) * EXPERT_TILE
        r = lax.broadcasted_iota(I32, (N_EXPERTS, N_EXPERTS), 0)
        c = lax.broadcasted_iota(I32, (N_EXPERTS, N_EXPERTS), 1)
        off_ref[...] = _dot((c < r).astype(F32), padded, HIGHEST)
        run_ref[...] = jnp.zeros_like(run_ref)

    @pl.when(phase == 1)
    def _():
        r = lax.broadcasted_iota(I32, (tb, tb), 0)
        c = lax.broadcasted_iota(I32, (tb, tb), 1)
        before = _dot(sel_all.astype(BF16), (r < c).astype(BF16))
        row = before + run_ref[:, 0:1] + off_ref[:, 0:1]
        es = [jnp.exp(vv - vals[0]) for vv in vals]
        tot = es[0] + es[1] + es[2] + es[3]
        for kk in range(TOP_K):
            pos_ref[kk:kk + 1, :] = jnp.sum(sels[kk] * row, axis=0, keepdims=True).astype(I32)
            wt_ref[kk:kk + 1, :] = es[kk] / tot
        run_ref[...] += jnp.broadcast_to(block_count, run_ref.shape)


def _router(logits):
    n = logits.shape[1]
    tb = max(t for t in range(LANES, ROUTER_BLOCK_MAX + 1, LANES) if n % t == 0)
    nb = n // tb
    return pl.pallas_call(
        _router_kernel,
        grid=(2, nb),
        in_specs=[pl.BlockSpec((N_EXPERTS, tb), lambda p, i: (0, i))],
        out_specs=(pl.BlockSpec((TOP_K, tb), lambda p, i: (0, i * p)),
                   pl.BlockSpec((TOP_K, tb), lambda p, i: (0, i * p)),
                   pl.BlockSpec((N_EXPERTS, LANES), lambda p, i: (0, 0))),
        out_shape=(jax.ShapeDtypeStruct((TOP_K, n), I32), jax.ShapeDtypeStruct((TOP_K, n), F32),
                   jax.ShapeDtypeStruct((N_EXPERTS, LANES), F32)),
        scratch_shapes=[pltpu.VMEM((N_EXPERTS, LANES), F32), pltpu.VMEM((N_EXPERTS, LANES), F32)],
        compiler_params=pltpu.CompilerParams(
            dimension_semantics=("arbitrary", "arbitrary"), vmem_limit_bytes=VMEM_LIMIT),
        name="router",
    )(logits)


def _row_dma_loop(n_tokens, make_copy, wait):
    def body(t0, carry):
        for u in range(ROW_DMA_UNROLL):
            for kk in range(TOP_K):
                copy = make_copy(t0 * ROW_DMA_UNROLL + u, kk)
                if wait:
                    copy.wait()
                else:
                    copy.start(priority=kk % 2)
        return carry

    lax.fori_loop(0, n_tokens // ROW_DMA_UNROLL, body, 0)


def _dispatch_kernel(clear_ref, pos_ref, h2_ref, xs_ref, zero_ref, sem, zsem):
    i = pl.program_id(0)
    tb = pos_ref.shape[1]
    tile_rows = EXPERT_TILE * ROW_TILES
    n_tiles = xs_ref.shape[0] // tile_rows

    @pl.when(i == 0)
    def _():
        zero_ref[...] = jnp.zeros_like(zero_ref)

        def tile_copy(t):
            start_row = pl.multiple_of(t * tile_rows, tile_rows)
            return pltpu.make_async_copy(zero_ref, xs_ref.at[pl.ds(start_row, tile_rows)], zsem)

        def start(t, carry):
            @pl.when(clear_ref[t] > 0)
            def _():
                tile_copy(t).start()
            return carry

        def wait(t, carry):
            @pl.when(clear_ref[t] > 0)
            def _():
                tile_copy(t).wait()
            return carry

        lax.fori_loop(0, n_tiles, start, 0)
        lax.fori_loop(0, n_tiles, wait, 0)

    row_copy = lambda t, kk: pltpu.make_async_copy(_row_tile(h2_ref, t), _row_tile(xs_ref, pos_ref[kk, t]), sem)
    _row_dma_loop(tb, row_copy, wait=False)
    _row_dma_loop(tb, row_copy, wait=True)


def _dispatch(h2_all, pos, clear, n_rows):
    n = h2_all.shape[0] // ROW_TILES
    tb = max(t for t in range(TOKEN_BLOCK, DISPATCH_BLOCK_MAX + 1, TOKEN_BLOCK) if n % t == 0)
    return pl.pallas_call(
        _dispatch_kernel,
        grid_spec=pltpu.PrefetchScalarGridSpec(
            num_scalar_prefetch=1,
            grid=(n // tb,),
            in_specs=[pl.BlockSpec((TOP_K, tb), lambda i, clr: (0, i), memory_space=pltpu.SMEM),
                      pl.BlockSpec((tb * ROW_TILES, LANES), lambda i, clr: (i, 0))],
            out_specs=pl.BlockSpec(memory_space=pl.ANY),
            scratch_shapes=[pltpu.VMEM((EXPERT_TILE * ROW_TILES, LANES), F32), pltpu.SemaphoreType.DMA,
                            pltpu.SemaphoreType.DMA],
        ),
        out_shape=jax.ShapeDtypeStruct((n_rows * ROW_TILES, LANES), F32),
        compiler_params=pltpu.CompilerParams(dimension_semantics=("arbitrary",), vmem_limit_bytes=VMEM_LIMIT),
        name="dispatch",
    )(clear, pos, h2_all)


def _expert_kernel(tile_e_ref, nvalid_ref, tile_rows_ref, first_ref, next_e_ref,
                   xs_ref, wg_hbm, bg_ref, wu_hbm, bu_ref, wd_hbm, bd_ref,
                   ys_ref, w_f32, wg_b, wu_b, wd_b, slot_ref, wsem):
    i = pl.program_id(0)
    valid_tile = i < nvalid_ref[0]
    tile_rows = tile_rows_ref[i]
    half = EXPERT_TILE // 2

    def weight_copies(e, s):
        return [pltpu.make_async_copy(w.at[e], w_f32.at[s, m], wsem.at[s, m])
                for m, w in enumerate((wg_hbm, wu_hbm, wd_hbm))]

    @pl.when(i == 0)
    def _():
        slot_ref[0] = 0
        for c in weight_copies(tile_e_ref[0], 0):
            c.start()

    @pl.when(jnp.logical_and(valid_tile, first_ref[i] > 0))
    def _():
        s = slot_ref[0]
        for c in weight_copies(tile_e_ref[i], s):
            c.wait()
        wg_b[...] = w_f32[s, 0].astype(BF16)
        wu_b[...] = w_f32[s, 1].astype(BF16)
        wd_b[...] = w_f32[s, 2].astype(BF16)
        slot_ref[0] = 1 - s

        @pl.when(next_e_ref[i] >= 0)
        def _():
            for c in weight_copies(next_e_ref[i], 1 - s):
                c.start()

    def ffn(rows):
        x = _load_rows(xs_ref, rows).astype(BF16)
        gate = jnp.minimum(_dot(x, wg_b[...]) + bg_ref[0], SWIGLU_LIMIT)
        up = jnp.clip(_dot(x, wu_b[...]) + bu_ref[0], -SWIGLU_LIMIT, SWIGLU_LIMIT)
        glu = gate * _sigmoid(SWIGLU_ALPHA * gate)
        return _dot(((up + 1.0) * glu).astype(BF16), wd_b[...]) + bd_ref[0]

    @pl.when(jnp.logical_and(valid_tile, tile_rows > half))
    def _():
        _store_rows(ys_ref, ffn(EXPERT_TILE))

    @pl.when(jnp.logical_and(valid_tile, tile_rows <= half))
    def _():
        _store_rows(ys_ref.at[pl.ds(0, half * ROW_TILES)], ffn(half))
        ys_ref[half * ROW_TILES:, :] = jnp.zeros((half * ROW_TILES, LANES), F32)

    @pl.when(i >= nvalid_ref[0])
    def _():
        ys_ref[...] = jnp.zeros_like(ys_ref)


def _experts(xs, tile_expert, n_valid, tile_rows, first, next_expert, w_gate, b_gate, w_up, b_up, w_down, b_down):
    d, d_ff = w_gate.shape[1], w_gate.shape[2]
    assert d == d_ff, "one [2, 3, d, d_ff] scratch holds all three f32 weight matrices"
    tm = EXPERT_TILE
    n_tiles = xs.shape[0] // (tm * ROW_TILES)
    row_map = lambda i, te, nv, *_: (jnp.minimum(i, nv[0] - 1), 0)
    b_map = lambda i, te, *_: (te[i], 0, 0)
    hbm = pl.BlockSpec(memory_space=pl.ANY)
    return pl.pallas_call(
        _expert_kernel,
        grid_spec=pltpu.PrefetchScalarGridSpec(
            num_scalar_prefetch=5,
            grid=(n_tiles,),
            in_specs=[pl.BlockSpec((tm * ROW_TILES, LANES), row_map),
                      hbm, pl.BlockSpec((1, 1, d_ff), b_map),
                      hbm, pl.BlockSpec((1, 1, d_ff), b_map),
                      hbm, pl.BlockSpec((1, 1, d), b_map)],
            out_specs=pl.BlockSpec((tm * ROW_TILES, LANES), lambda i, *_: (i, 0)),
            scratch_shapes=[pltpu.VMEM((2, 3, d, d_ff), F32),
                            pltpu.VMEM((d, d_ff), BF16), pltpu.VMEM((d, d_ff), BF16), pltpu.VMEM((d_ff, d), BF16),
                            pltpu.SMEM((1,), I32), pltpu.SemaphoreType.DMA((2, 3))],
        ),
        out_shape=jax.ShapeDtypeStruct(xs.shape, F32),
        compiler_params=pltpu.CompilerParams(dimension_semantics=("arbitrary",), vmem_limit_bytes=VMEM_LIMIT),
        name="experts",
    )(tile_expert, n_valid, tile_rows, first, next_expert, xs, w_gate, b_gate[:, None, :], w_up, b_up[:, None, :],
      w_down, b_down[:, None, :])


def _combine_kernel(pos_ref, pos_next_ref, x1_ref, wt_ref, g_final_ref, ys_ref, yp_ref, ysmp_ref, buf_ref, sem):
    i = pl.program_id(0)
    last = pl.num_programs(0) - 1
    tb = x1_ref.shape[0]
    slot = i % 2

    def row_copy(block_pos_ref, s):
        return lambda t, kk: pltpu.make_async_copy(
            _row_tile(ys_ref, block_pos_ref[kk, t]), _row_tile(buf_ref.at[s, kk], t), sem.at[s])

    @pl.when(i == 0)
    def _():
        _row_dma_loop(tb, row_copy(pos_ref, 0), wait=False)

    @pl.when(i < last)
    def _():
        _row_dma_loop(tb, row_copy(pos_next_ref, 1 - slot), wait=False)

    _row_dma_loop(tb, row_copy(pos_ref, slot), wait=True)
    out = x1_ref[...]
    for kk in range(TOP_K):
        out = out + wt_ref[:, kk:kk + 1] * _load_rows(buf_ref.at[slot, kk], tb)
    y = _rms(out, g_final_ref[...])

    @pl.when(i < last)
    def _():
        yp_ref[0] = y

    @pl.when(i == last)
    def _():
        ysmp_ref[...] = y[0:ysmp_ref.shape[0], :]


def _combine(x1_all, pos, wt_cols, ys, g_final_row, batch, seq, ns):
    d = x1_all.shape[1]
    tb = TOKEN_BLOCK
    nt = seq // tb
    n_prompt_blocks = batch * nt
    nb = n_prompt_blocks + 1

    def prompt_block_index(i):
        i = jnp.minimum(i, n_prompt_blocks - 1)
        step, s = i // SEQS_PER_STEP, i % SEQS_PER_STEP
        return ((step // nt) * SEQS_PER_STEP + s, step % nt, 0)

    return pl.pallas_call(
        _combine_kernel,
        grid=(nb,),
        in_specs=[pl.BlockSpec((TOP_K, tb), lambda i: (0, i), memory_space=pltpu.SMEM),
                  pl.BlockSpec((TOP_K, tb), lambda i: (0, jnp.minimum(i + 1, nb - 1)), memory_space=pltpu.SMEM),
                  pl.BlockSpec((tb, d), lambda i: (i, 0)),
                  pl.BlockSpec((tb, TOP_K), lambda i: (i, 0)),
                  pl.BlockSpec((1, d), lambda i: (0, 0)),
                  pl.BlockSpec(memory_space=pl.ANY)],
        out_specs=(pl.BlockSpec((1, tb, d), prompt_block_index),
                   pl.BlockSpec((ns, d), lambda i: (0, 0))),
        out_shape=(jax.ShapeDtypeStruct((batch, seq, d), F32), jax.ShapeDtypeStruct((ns, d), F32)),
        scratch_shapes=[pltpu.VMEM((2, TOP_K, tb * ROW_TILES, LANES), F32), pltpu.SemaphoreType.DMA((2,))],
        compiler_params=pltpu.CompilerParams(dimension_semantics=("arbitrary",), vmem_limit_bytes=VMEM_LIMIT),
        name="combine",
    )(pos, pos, x1_all, wt_cols, g_final_row, ys)


def _prepare_weights(g_mix, w_in, conv_w, conv_b, b_igate, b_fgate, g_mhnorm, w_a, w_spatial, b_spatial, g_vnorm,
                     w_b, w_out, g_ffn, w_router, b_router):
    d = w_in.shape[0]
    gate_lo = 2 * D_QK + D_VAL
    gate_hi = gate_lo + GATE_COLS
    row = lambda a: a.reshape(1, -1).astype(F32)
    w_if = w_in[:, gate_lo:gate_hi]
    gate_b = jnp.concatenate([b_igate, b_fgate])
    tril = jnp.tril(jnp.ones((SPATIAL_CHUNK, SPATIAL_CHUNK), w_spatial.dtype))
    chunks = PROMPT_BLOCK // SPATIAL_CHUNK
    head_of_lane = jnp.arange(D_QK) // D_K
    seg = (head_of_lane[:, None] == jnp.arange(LANES)[None, :]).astype(F32)
    pad_cols = lambda a: jnp.pad(a, ((0, 0), (0, LANES - N_HEADS)))
    return dict(
        g_mix=row(g_mix),
        w_lo=w_in[:, :gate_lo].astype(BF16), w_hi=w_in[:, gate_hi:].astype(BF16),
        w_ig=pad_cols(w_if[:, :N_HEADS]).astype(BF16),
        w_fg=pad_cols(w_if[:, N_HEADS:]).astype(BF16),
        w_ift=w_if.T.astype(BF16),
        w_gates=jnp.pad(w_if, ((0, 0), (0, LANES - GATE_COLS))).astype(BF16),
        b_gates=jnp.pad(gate_b, (0, LANES - GATE_COLS)).reshape(1, LANES),
        b_ig=pad_cols(b_igate.reshape(1, N_HEADS)),
        b_fg=pad_cols(b_fgate.reshape(1, N_HEADS)),
        gate_b_col=jnp.broadcast_to(gate_b[:, None], (GATE_COLS, LANES)),
        conv_w=conv_w, conv_b=row(conv_b), g_mh=row(g_mhnorm), w_a=w_a.astype(BF16),
        ws_masked=jnp.stack([jnp.kron(jnp.eye(chunks, dtype=w_spatial.dtype), w_g)
                             for w_g in w_spatial * tril]).astype(BF16),
        bsp_full=jnp.tile(jnp.repeat(b_spatial.T, GROUP_DIM, axis=1), (chunks, 1)),
        wsp_row=jnp.repeat(w_spatial[:, 0, 0], GROUP_DIM).reshape(1, -1),
        bsp_row=jnp.repeat(b_spatial[:, 0], GROUP_DIM).reshape(1, -1),
        g_vn=row(g_vnorm), w_b=w_b.astype(BF16), w_out=w_out.astype(BF16), g_ffn=row(g_ffn),
        w_rt=w_router.T, b_r_col=jnp.broadcast_to(b_router[:, None], (N_EXPERTS, LANES)),
        seg=seg, segt=seg.T,
    )


def _tile_metadata(counts, n_tiles):
    tiles_per_expert = (counts + EXPERT_TILE - 1) // EXPERT_TILE
    tile_end = jnp.cumsum(tiles_per_expert)
    n_valid = tile_end[-1]
    tile_ids = jnp.minimum(jnp.arange(n_tiles, dtype=I32), n_valid - 1)
    tile_expert = jnp.sum(tile_end[None, :] <= tile_ids[:, None], axis=1).astype(I32)
    tail_tile = jnp.where(counts > 0, tile_end - 1, -1)
    all_tiles = jnp.arange(n_tiles, dtype=I32)
    clear = jnp.logical_or(all_tiles >= n_valid, jnp.any(all_tiles[:, None] == tail_tile[None, :], axis=1))
    expert_of = tile_expert[:, None] == jnp.arange(counts.shape[0], dtype=I32)[None, :]
    pick = lambda per_expert: jnp.sum(jnp.where(expert_of, per_expert[None, :], 0), axis=1)
    tile_in_expert = tile_ids - pick(tile_end - tiles_per_expert)
    tile_rows = jnp.clip(pick(counts) - tile_in_expert * EXPERT_TILE, 0, EXPERT_TILE).astype(I32)
    first = jnp.logical_and(all_tiles < n_valid, tile_in_expert == 0).astype(I32)
    following_tile = pick(tile_end)
    next_expert = jnp.sum(jnp.where(all_tiles[None, :] == following_tile[:, None], tile_expert[None, :], 0), axis=1)
    next_expert = jnp.where(following_tile < n_valid, next_expert, -1).astype(I32)
    return tile_expert, n_valid.reshape(1).astype(I32), clear.astype(I32), tile_rows, first, next_expert


def kernel(x_prompt, x_sample, state_conv, state_C, state_n, state_m, g_mix, w_in, conv_w, conv_b, b_igate, b_fgate,
           g_mhnorm, w_a, w_spatial, b_spatial, g_vnorm, w_b, w_out, g_ffn, w_router, b_router, w_gate, b_gate,
           w_up, b_up, w_down, b_down, g_final):
    depth = g_mix.shape[0]
    assert depth == 1, "single-layer trunk"
    batch, seq, d = x_prompt.shape
    ns = x_sample.shape[0]
    assert seq % PROMPT_BLOCK == 0 and ns <= TOKEN_BLOCK and PROMPT_BLOCK == TOKEN_BLOCK
    assert batch % SEQS_PER_STEP == 0
    assert PROMPT_BLOCK % MLSTM_CHUNK == 0 and MLSTM_CHUNK % SPATIAL_CHUNK == 0
    wts = _prepare_weights(g_mix[0], w_in[0], conv_w[0], conv_b[0], b_igate[0], b_fgate[0], g_mhnorm[0], w_a[0],
                           w_spatial[0], b_spatial[0], g_vnorm[0], w_b[0], w_out[0], g_ffn[0], w_router[0],
                           b_router[0])

    x1_s, h2_s, logits_s, s_conv, s_c, s_n, s_m, vg = _sample_mixer(
        x_sample, state_conv[0], state_C[0], state_n[0], state_m[0], wts)
    x1_all, h2_all, logits, p_conv, p_c, p_n, p_m = _prompt_mixer(x_prompt, x1_s, h2_s, logits_s, wts)
    n_total = x1_all.shape[0]

    pos, wt, counts = _router(logits)
    n_tiles = -(-(n_total * TOP_K) // EXPERT_TILE) + N_EXPERTS
    tile_expert, n_valid, clear, tile_rows, first, next_expert = _tile_metadata(counts[:, 0].astype(I32), n_tiles)
    xs = _dispatch(h2_all, pos, clear, n_tiles * EXPERT_TILE)
    ys = _experts(xs, tile_expert, n_valid, tile_rows, first, next_expert, w_gate[0], b_gate[0], w_up[0], b_up[0],
                  w_down[0], b_down[0])
    y_prompt, y_sample = _combine(x1_all, pos, wt.T, ys, g_final.reshape(1, d), batch, seq, ns)

    return (y_prompt, y_sample.reshape(ns, 1, d),
            p_conv[None], p_c[None], p_n[None], p_m[None, :, :, 0],
            s_conv[None], s_c[None], s_n[None], s_m[None], vg.reshape(1, ns, 1, N_GROUPS, GROUP_DIM))
```
